```python
import math
import jax, jax.numpy as jnp
from jax import lax
import numpy as np


D_MODEL = 1024
BATCH = 2
SEQ = 8192
DEPTH = 4

HEAD_DIM = 64
ROPE_THETA = 10000.0
EPS = 1e-6
A_Q_HEADS = 8
A_KV_HEADS = 2
A_GROUP = A_Q_HEADS // A_KV_HEADS
A_WIDTH = A_Q_HEADS * HEAD_DIM
A_KV_WIDTH = A_KV_HEADS * HEAD_DIM
WINDOW = 128
A_BLOCK = 128
B_HEADS = 4
B_QK_DIM = HEAD_DIM
B_V_DIM = 2 * HEAD_DIM
B_QK_WIDTH = B_HEADS * 2 * B_QK_DIM
B_WIDTH = B_HEADS * B_V_DIM
B_QBLOCK = 128
C_HEADS = 4
C_HEAD_DIM = 128
C_WIDTH = C_HEADS * C_HEAD_DIM
C_CONV = 5
C_CHUNK = 64
N_BRANCH = 3

IN_SPLITS = (A_WIDTH, A_KV_WIDTH, A_KV_WIDTH, A_WIDTH,
             B_QK_WIDTH, B_QK_WIDTH, B_WIDTH, B_WIDTH,
             3 * C_WIDTH, C_WIDTH, 2 * C_HEADS, 2 * C_HEADS,
             N_BRANCH * D_MODEL)
D_IN = sum(IN_SPLITS)

kernel_name = 'hybrid_gated_branch_encoder'


def rmsnorm(x, g):
    x32 = x.astype(jnp.float32)
    y = x32 * lax.rsqrt(jnp.mean(x32 * x32, axis=-1, keepdims=True) + EPS)
    return (y * g.astype(jnp.float32)).astype(x.dtype)


def l2norm(x):
    return x * lax.rsqrt(jnp.sum(x * x, axis=-1, keepdims=True) + EPS)


def rope_tables(seq, dim):
    inv = 1.0 / (ROPE_THETA ** (jnp.arange(0, dim, 2, dtype=jnp.float32) / dim))
    ang = jnp.arange(seq, dtype=jnp.float32)[:, None] * inv[None, :]
    return jnp.cos(ang), jnp.sin(ang)


def apply_rope(t, cos, sin):
    shape = (t.shape[1],) + (1,) * (t.ndim - 3) + (cos.shape[-1],)
    c = cos.reshape(shape)
    s = sin.reshape(shape)
    t32 = t.astype(jnp.float32)
    t1, t2 = jnp.split(t32, 2, axis=-1)
    return jnp.concatenate([t1 * c - t2 * s, t2 * c + t1 * s], axis=-1).astype(t.dtype)


def windowed_gqa(q, k, v, sink, cos, sin):
    bsz, seq = q.shape[0], q.shape[1]
    nb = seq // A_BLOCK
    q = apply_rope(q, cos, sin)
    k = apply_rope(k, cos, sin)
    qb = q.reshape(bsz, nb, A_BLOCK, A_KV_HEADS, A_GROUP, HEAD_DIM)

    def band(t):
        tb = t.reshape(bsz, nb, A_BLOCK, A_KV_HEADS, HEAD_DIM)
        tp = jnp.pad(tb, ((0, 0), (1, 1), (0, 0), (0, 0), (0, 0)))
        return jnp.concatenate([tp[:, :-2], tp[:, 1:-1], tp[:, 2:]], axis=2)

    kw, vw = band(k), band(v)
    s = jnp.einsum('bnqhgd,bnkhd->bnhgqk', qb, kw).astype(jnp.float32) * (HEAD_DIM ** -0.5)
    qpos = jnp.arange(A_BLOCK)[:, None]
    kpos = jnp.arange(3 * A_BLOCK)[None, :] - A_BLOCK
    kabs = jnp.arange(nb)[:, None, None] * A_BLOCK + kpos[None]
    valid = (jnp.abs(kpos - qpos) <= WINDOW)[None] & (kabs >= 0) & (kabs < seq)
    s = jnp.where(valid[None, :, None, None], s, -1e30)
    sink_col = jnp.broadcast_to(sink.astype(jnp.float32).reshape(1, 1, A_KV_HEADS, A_GROUP, 1, 1),
                                s.shape[:-1] + (1,))
    p = jax.nn.softmax(jnp.concatenate([s, sink_col], axis=-1), axis=-1)[..., :-1]
    o = jnp.einsum('bnhgqk,bnkhd->bnqhgd', p.astype(v.dtype), vw)
    return o.reshape(bsz, seq, A_WIDTH)


def diff_attention(q, k, v, lam, sub_g, lam_init, cos, sin):
    bsz, seq = q.shape[0], q.shape[1]
    nb = seq // B_QBLOCK
    q = apply_rope(q, cos, sin)
    k = apply_rope(k, cos, sin)
    qb = q.reshape(bsz, nb, B_QBLOCK, B_HEADS, 2, B_QK_DIM).transpose(1, 0, 2, 3, 4, 5)
    scale = B_QK_DIM ** -0.5

    def block(qi):
        s = jnp.einsum('bqhcd,bkhcd->bhcqk', qi, k).astype(jnp.float32) * scale
        p = jax.nn.softmax(s, axis=-1)
        a = p[:, :, 0] - lam * p[:, :, 1]
        return jnp.einsum('bhqk,bkhe->bqhe', a.astype(v.dtype), v)

    o = lax.map(block, qb)
    o = o.transpose(1, 0, 2, 3, 4).reshape(bsz, seq, B_HEADS, B_V_DIM)
    o = rmsnorm(o, sub_g) * (1.0 - lam_init)
    return o.reshape(bsz, seq, B_WIDTH)


def centred_short_conv(u, w):
    pad = (C_CONV - 1) // 2
    y = lax.conv_general_dilated(u, w[:, None, :].astype(u.dtype), window_strides=(1,),
                                 padding=[(pad, pad)], dimension_numbers=('NWC', 'WIO', 'NWC'),
                                 feature_group_count=u.shape[-1])
    return jax.nn.silu(y)


def gated_delta_chunked(q, k, v, g, beta):
    bsz, seq, nh, dk = q.shape
    dv = v.shape[-1]
    n = seq // C_CHUNK
    q = l2norm(q) * (dk ** -0.5)
    k = l2norm(k)

    def chunk(t):
        return t.reshape(bsz, n, C_CHUNK, nh, t.shape[-1]).transpose(0, 3, 1, 2, 4)

    qc, kc, vc = chunk(q), chunk(k), chunk(v)
    gc = jnp.cumsum(g.reshape(bsz, n, C_CHUNK, nh).transpose(0, 3, 1, 2), axis=-1)
    bc = beta.reshape(bsz, n, C_CHUNK, nh).transpose(0, 3, 1, 2)[..., None]
    tril = jnp.tril(jnp.ones((C_CHUNK, C_CHUNK), dtype=bool))
    strict = jnp.tril(jnp.ones((C_CHUNK, C_CHUNK), dtype=bool), -1)
    diff = gc[..., :, None] - gc[..., None, :]
    decay = jnp.where(tril, jnp.exp(jnp.where(tril, diff, 0.0)), 0.0)
    kb = kc * bc
    vb = vc * bc
    amat = jnp.where(strict, jnp.einsum('bhncd,bhnjd->bhncj', kb, kc) * decay, 0.0)
    eye = jnp.eye(C_CHUNK, dtype=jnp.float32)
    tmat = lax.linalg.triangular_solve(amat + eye, jnp.broadcast_to(eye, amat.shape),
                                       left_side=True, lower=True)
    u = jnp.einsum('bhncj,bhnje->bhnce', tmat, vb)
    w = jnp.einsum('bhncj,bhnjd->bhncd', tmat, kb * jnp.exp(gc)[..., None])
    attn = jnp.where(tril, jnp.einsum('bhncd,bhnjd->bhncj', qc, kc) * decay, 0.0)
    g_last = gc[..., -1]
    k_tail = kc * jnp.exp(g_last[..., None] - gc)[..., None]
    q_dec = qc * jnp.exp(gc)[..., None]

    def to_front(t):
        return jnp.moveaxis(t, 2, 0)

    xs = (to_front(q_dec), to_front(w), to_front(u), to_front(attn), to_front(k_tail), to_front(g_last))

    def step(state, inp):
        qd, wi, ui, ai, kt, gl = inp
        v_new = ui - jnp.einsum('bhcd,bhde->bhce', wi, state)
        o = jnp.einsum('bhcd,bhde->bhce', qd, state) + jnp.einsum('bhcj,bhje->bhce', ai, v_new)
        state = state * jnp.exp(gl)[..., None, None] + jnp.einsum('bhcd,bhce->bhde', kt, v_new)
        return state, o

    state0 = jnp.zeros((bsz, nh, dk, dv), dtype=jnp.float32)
    _, o = lax.scan(step, state0, xs)
    return o.transpose(1, 0, 3, 2, 4).reshape(bsz, seq, nh, dv)


def setup_inputs(seed: int = 0) -> dict:
    key = jax.random.key(seed)
    ks = jax.random.split(key, 16)
    f32 = jnp.float32
    x = jax.random.normal(ks[0], (BATCH, SEQ, D_MODEL), f32)
    norm_g = 1.0 + 0.05 * jax.random.normal(ks[1], (DEPTH, D_MODEL), f32)
    w_in = jax.random.normal(ks[2], (DEPTH, D_MODEL, D_IN), f32) * (D_MODEL ** -0.5)
    a_sink = 0.5 * jax.random.normal(ks[3], (DEPTH, A_Q_HEADS), f32)
    b_lambda = 0.1 * jax.random.normal(ks[4], (DEPTH, 4, B_QK_DIM), f32)
    b_subln_g = 1.0 + 0.05 * jax.random.normal(ks[5], (DEPTH, B_V_DIM), f32)
    c_conv_w = jax.random.normal(ks[6], (DEPTH, C_CONV, 3 * C_WIDTH), f32) * (C_CONV ** -0.5)
    c_a_log = jnp.log(jax.random.uniform(ks[7], (DEPTH, 2, C_HEADS), f32, 1.0, 16.0))
    dt = jnp.exp(jax.random.uniform(ks[8], (DEPTH, 2, C_HEADS), f32, math.log(1e-3), math.log(1e-1)))
    c_dt_bias = dt + jnp.log(-jnp.expm1(-dt))
    c_norm_g = 1.0 + 0.05 * jax.random.normal(ks[9], (DEPTH, C_HEAD_DIM), f32)
    w_bo_a = jax.random.normal(ks[10], (DEPTH, A_WIDTH, D_MODEL), f32) * (A_WIDTH ** -0.5)
    w_bo_b = jax.random.normal(ks[11], (DEPTH, B_WIDTH, D_MODEL), f32) * (B_WIDTH ** -0.5)
    w_bo_c = jax.random.normal(ks[12], (DEPTH, C_WIDTH, D_MODEL), f32) * (C_WIDTH ** -0.5)
    w_out = jax.random.normal(ks[13], (DEPTH, D_MODEL, D_MODEL), f32) * (0.5 * D_MODEL ** -0.5)
    final_g = 1.0 + 0.05 * jax.random.normal(ks[14], (D_MODEL,), f32)
    return {'x': x, 'norm_g': norm_g, 'w_in': w_in, 'a_sink': a_sink, 'b_lambda': b_lambda,
            'b_subln_g': b_subln_g, 'c_conv_w': c_conv_w, 'c_a_log': c_a_log, 'c_dt_bias': c_dt_bias,
            'c_norm_g': c_norm_g, 'w_bo_a': w_bo_a, 'w_bo_b': w_bo_b, 'w_bo_c': w_bo_c,
            'w_out': w_out, 'final_g': final_g}


def reference(x, norm_g, w_in, a_sink, b_lambda, b_subln_g, c_conv_w, c_a_log, c_dt_bias,
              c_norm_g, w_bo_a, w_bo_b, w_bo_c, w_out, final_g):
    bsz, seq = x.shape[0], x.shape[1]
    cos, sin = rope_tables(seq, HEAD_DIM)
    offsets = [int(o) for o in np.cumsum(IN_SPLITS)[:-1]]
    for l in range(DEPTH):
        h = rmsnorm(x, norm_g[l])
        proj = jnp.einsum('bsd,de->bse', h, w_in[l])
        (a_q, a_k, a_v, a_z, b_q, b_k, b_v, b_z,
         c_qkv, c_z, c_b, c_a, gate_logits) = jnp.split(proj, offsets, axis=-1)

        o_a = windowed_gqa(a_q.reshape(bsz, seq, A_Q_HEADS, HEAD_DIM),
                           a_k.reshape(bsz, seq, A_KV_HEADS, HEAD_DIM),
                           a_v.reshape(bsz, seq, A_KV_HEADS, HEAD_DIM), a_sink[l], cos, sin)
        y_a = jnp.einsum('bse,ed->bsd', o_a * jax.nn.silu(a_z), w_bo_a[l])

        lam_init = 0.8 - 0.6 * math.exp(-0.3 * l)
        lq1, lk1, lq2, lk2 = b_lambda[l].astype(jnp.float32)
        lam = jnp.exp(jnp.sum(lq1 * lk1)) - jnp.exp(jnp.sum(lq2 * lk2)) + lam_init
        o_b = diff_attention(b_q.reshape(bsz, seq, B_HEADS, 2, B_QK_DIM),
                             b_k.reshape(bsz, seq, B_HEADS, 2, B_QK_DIM),
                             b_v.reshape(bsz, seq, B_HEADS, B_V_DIM), lam, b_subln_g[l], lam_init, cos, sin)
        y_b = jnp.einsum('bse,ed->bsd', o_b * jax.nn.silu(b_z), w_bo_b[l])

        qkv = centred_short_conv(c_qkv, c_conv_w[l]).astype(jnp.float32)
        c_q, c_k, c_v = jnp.split(qkv, 3, axis=-1)
        c_q = c_q.reshape(bsz, seq, C_HEADS, C_HEAD_DIM)
        c_k = c_k.reshape(bsz, seq, C_HEADS, C_HEAD_DIM)
        c_v = c_v.reshape(bsz, seq, C_HEADS, C_HEAD_DIM)
        beta = jax.nn.sigmoid(c_b.astype(jnp.float32)).reshape(bsz, seq, 2, C_HEADS)
        g = -jnp.exp(c_a_log[l].astype(jnp.float32))[None, None] * jax.nn.softplus(
            c_a.astype(jnp.float32).reshape(bsz, seq, 2, C_HEADS) + c_dt_bias[l].astype(jnp.float32)[None, None])
        o_fwd = gated_delta_chunked(c_q, c_k, c_v, g[:, :, 0], beta[:, :, 0])
        o_bwd = jnp.flip(gated_delta_chunked(jnp.flip(c_q, 1), jnp.flip(c_k, 1), jnp.flip(c_v, 1),
                                             jnp.flip(g[:, :, 1], 1), jnp.flip(beta[:, :, 1], 1)), 1)
        o_c = rmsnorm((o_fwd + o_bwd).astype(x.dtype), c_norm_g[l]).reshape(bsz, seq, C_WIDTH)
        y_c = jnp.einsum('bse,ed->bsd', o_c * jax.nn.silu(c_z), w_bo_c[l])

        gates = jax.nn.sigmoid(gate_logits.astype(jnp.float32)).reshape(bsz, seq, N_BRANCH, D_MODEL).astype(x.dtype)
        merged = gates[:, :, 0] * y_a + gates[:, :, 1] * y_b + gates[:, :, 2] * y_c
        x = x + jnp.einsum('bsd,de->bse', merged, w_out[l])
    return rmsnorm(x, final_g)
```

```python
import functools
import math

import numpy as np
import jax
import jax.numpy as jnp
from jax import lax
from jax.experimental import pallas as pl
from jax.experimental.pallas import tpu as pltpu

F32 = jnp.float32
BF16 = jnp.bfloat16

D_MODEL = 1024
HEAD_DIM = 64
ROPE_THETA = 10000.0
EPS = 1e-6
A_Q_HEADS = 8
A_KV_HEADS = 2
A_WIDTH = A_Q_HEADS * HEAD_DIM
A_KV_WIDTH = A_KV_HEADS * HEAD_DIM
WINDOW = 128
B_HEADS = 4
B_V_DIM = 2 * HEAD_DIM
B_WIDTH = B_HEADS * B_V_DIM
C_HEADS = 4
C_HEAD_DIM = 128
C_WIDTH = C_HEADS * C_HEAD_DIM
C_CONV = 5
C_CHUNK = 64
N_BRANCH = 3
LANES = 128
SUBLANES = 8
VMEM_LIMIT = 56 * 1024 * 1024

IN_SPLITS = (A_WIDTH, A_KV_WIDTH, A_KV_WIDTH, A_WIDTH,
             B_WIDTH, B_WIDTH, B_WIDTH, B_WIDTH,
             3 * C_WIDTH, C_WIDTH, 2 * C_HEADS, 2 * C_HEADS,
             N_BRANCH * D_MODEL)
_OFF = [0] + [int(o) for o in np.cumsum(IN_SPLITS)]

_PERM_A = np.concatenate([np.concatenate([np.arange(64 * b, 64 * b + 64),
                                          np.arange(64 * (4 + b), 64 * (4 + b) + 64)])
                          for b in range(4)])
_SEC = {}
_cols = []


def _add_section(name, idx):
    _SEC[name] = (sum(len(c) for c in _cols), len(idx))
    _cols.append(np.asarray(idx))


_add_section('aq', _OFF[0] + _PERM_A)
_add_section('ak', np.arange(_OFF[1], _OFF[2]))
_add_section('av', np.arange(_OFF[2], _OFF[3]))
_add_section('az', _OFF[3] + _PERM_A)
_add_section('bq', np.arange(_OFF[4], _OFF[5]))
_add_section('bk', np.arange(_OFF[5], _OFF[6]))
_add_section('bv', np.arange(_OFF[6], _OFF[7]))
_add_section('bz', np.arange(_OFF[7], _OFF[8]))
_add_section('cqkv', np.arange(_OFF[8], _OFF[9]))
_add_section('cz', np.arange(_OFF[9], _OFF[10]))
_add_section('gates', np.arange(_OFF[12], _OFF[13]))
_add_section('cba', np.arange(_OFF[10], _OFF[12]))
_COL_IDX = np.concatenate(_cols)
N_BA = 4 * C_HEADS
N_PROJ = len(_COL_IDX) - N_BA


def _cparams(sem):
    return pltpu.CompilerParams(dimension_semantics=sem, vmem_limit_bytes=VMEM_LIMIT)


def _rope(t, cos, sin_signed, first_half):
    swapped = jnp.where(first_half, pltpu.roll(t, LANES - 32, 1), pltpu.roll(t, 32, 1))
    return t * cos + swapped * sin_signed


def _softplus(t):
    return jnp.maximum(t, 0.0) + jnp.log(1.0 + jnp.exp(-jnp.abs(t)))


def _inproj_kernel(x_ref, g_ref, w_ref, wba_ref, wbat_ref, cos_ref, sin_ref, dec_ref, dect_ref,
                   qa_ref, ka_ref, va_ref, za_ref, qb_ref, kb_ref, vb_ref, zb_ref,
                   cqkv_ref, zc_ref, gates_ref, bg_ref, bgt_ref):
    x = x_ref[...]
    h = x * lax.rsqrt(jnp.mean(x * x, axis=-1, keepdims=True) + EPS) * g_ref[...]
    hb = h.astype(BF16)
    cos = cos_ref[...]
    sin = sin_ref[...]
    first_half = (lax.broadcasted_iota(jnp.int32, cos.shape, 1) & 32) == 0

    def proj(name, blk):
        lo = _SEC[name][0] + blk * LANES
        return jnp.dot(hb, w_ref[:, lo:lo + LANES], preferred_element_type=F32)

    def nblk(name):
        return _SEC[name][1] // LANES

    for name, ref in (('aq', qa_ref), ('ak', ka_ref), ('bq', qb_ref), ('bk', kb_ref)):
        for j in range(nblk(name)):
            ref[:, j * LANES:(j + 1) * LANES] = _rope(proj(name, j), cos, sin, first_half).astype(BF16)
    for name, ref in (('av', va_ref), ('bv', vb_ref)):
        for j in range(nblk(name)):
            ref[:, j * LANES:(j + 1) * LANES] = proj(name, j).astype(BF16)
    for name, ref in (('az', za_ref), ('bz', zb_ref), ('cz', zc_ref)):
        for j in range(nblk(name)):
            t = proj(name, j)
            ref[:, j * LANES:(j + 1) * LANES] = (t * jax.nn.sigmoid(t)).astype(BF16)
    for j in range(nblk('gates')):
        gates_ref[:, j * LANES:(j + 1) * LANES] = jax.nn.sigmoid(proj('gates', j)).astype(BF16)
    for j in range(nblk('cqkv')):
        cqkv_ref[:, j * LANES:(j + 1) * LANES] = proj('cqkv', j)

    ba = jnp.dot(hb, wba_ref[...], preferred_element_type=F32)
    bat = lax.dot_general(wbat_ref[...], hb, (((1,), (1,)), ((), ())),
                          preferred_element_type=F32)
    is_beta = lax.broadcasted_iota(jnp.int32, ba.shape, 1) < 2 * C_HEADS
    dec = dec_ref[...]
    bg_ref[...] = jnp.where(is_beta, jax.nn.sigmoid(ba), dec[0:1, :] * _softplus(ba + dec[1:2, :]))
    is_beta_t = lax.broadcasted_iota(jnp.int32, bat.shape, 0) < 2 * C_HEADS
    dect = dect_ref[...]
    bgt_ref[...] = jnp.where(is_beta_t, jax.nn.sigmoid(bat),
                             dect[:, 0:1] * _softplus(bat + dect[:, 1:2]))


def _inproj(x2, g, w, wba, wbat, cos, sin, dec, dect, seq, tm):
    rows = x2.shape[0]
    nseq = seq // tm
    row = lambda i: (i, 0)
    const = lambda i: (0, 0)
    single = pl.Buffered(1)
    out_widths = [('aq', BF16), ('ak', BF16), ('av', BF16), ('az', BF16), ('bq', BF16), ('bk', BF16),
                  ('bv', BF16), ('bz', BF16), ('cqkv', F32), ('cz', BF16), ('gates', BF16)]
    out_shape = [jax.ShapeDtypeStruct((rows, _SEC[n][1]), dt) for n, dt in out_widths]
    out_specs = [pl.BlockSpec((tm, _SEC[n][1]), row) for n, _ in out_widths]
    out_shape += [jax.ShapeDtypeStruct((rows, N_BA), F32), jax.ShapeDtypeStruct((N_BA, rows), F32)]
    out_specs += [pl.BlockSpec((tm, N_BA), row), pl.BlockSpec((N_BA, tm), lambda i: (0, i))]
    return pl.pallas_call(
        _inproj_kernel,
        grid=(rows // tm,),
        in_specs=[pl.BlockSpec((tm, D_MODEL), row),
                  pl.BlockSpec((1, D_MODEL), const),
                  pl.BlockSpec((D_MODEL, N_PROJ), const, pipeline_mode=single),
                  pl.BlockSpec((D_MODEL, N_BA), const),
                  pl.BlockSpec((N_BA, D_MODEL), const),
                  pl.BlockSpec((tm, LANES), lambda i: (i % nseq, 0)),
                  pl.BlockSpec((tm, LANES), lambda i: (i % nseq, 0)),
                  pl.BlockSpec((2, N_BA), const),
                  pl.BlockSpec((N_BA, 2), const)],
        out_specs=out_specs,
        out_shape=out_shape,
        compiler_params=_cparams(("parallel",)),
    )(x2, g, w, wba, wbat, cos, sin, dec, dect)


def _attn_a_kernel(sink_ref, q_ref, kp_ref, k_ref, kn_ref, vp_ref, v_ref, vn_ref, z_ref, o_ref, *, tq, seq):
    i = pl.program_id(1)
    kcat = jnp.concatenate([kp_ref[0], k_ref[0], kn_ref[0]], axis=0)
    vcat = jnp.concatenate([vp_ref[0], v_ref[0], vn_ref[0]], axis=0)
    nk = tq + 2 * WINDOW
    lane = lax.broadcasted_iota(jnp.int32, (1, LANES), 1)
    low = lane < HEAD_DIM
    v_lo = jnp.where(low, vcat, jnp.zeros_like(vcat))
    v_hi = jnp.where(low, jnp.zeros_like(vcat), vcat)
    r = lax.broadcasted_iota(jnp.int32, (tq, nk), 0)
    c = lax.broadcasted_iota(jnp.int32, (tq, nk), 1)
    kabs = i * tq - WINDOW + c
    valid = (jnp.abs(c - WINDOW - r) <= WINDOW) & (kabs >= 0) & (kabs < seq)

    def probs(qm, sink):
        s = lax.dot_general(qm, kcat, (((1,), (1,)), ((), ())), preferred_element_type=F32)
        s = jnp.where(valid, s, -1e30)
        m = jnp.maximum(jnp.max(s, axis=-1, keepdims=True), sink)
        p = jnp.exp(s - m)
        l = jnp.sum(p, axis=-1, keepdims=True) + jnp.exp(sink - m)
        return (p / l).astype(BF16)

    for b in range(A_Q_HEADS // 2):
        q = q_ref[0, :, b * LANES:(b + 1) * LANES]
        p0 = probs(jnp.where(low, q, jnp.zeros_like(q)), sink_ref[b])
        p1 = probs(jnp.where(low, jnp.zeros_like(q), q), sink_ref[A_Q_HEADS // 2 + b])
        o = (jnp.dot(p0, v_lo, preferred_element_type=F32)
             + jnp.dot(p1, v_hi, preferred_element_type=F32))
        o_ref[0, :, b * LANES:(b + 1) * LANES] = (
            o * z_ref[0, :, b * LANES:(b + 1) * LANES].astype(F32)).astype(BF16)


def _attn_a(sink, q, k, v, z, tq):
    bsz, seq, _ = q.shape
    r = tq // WINDOW
    nwb = seq // WINDOW
    main = lambda b, i: (b, i, 0)
    prev = lambda b, i: (b, jnp.maximum(i * r - 1, 0), 0)
    nxt = lambda b, i: (b, jnp.minimum((i + 1) * r, nwb - 1), 0)
    kv_specs = [pl.BlockSpec((1, WINDOW, LANES), prev), pl.BlockSpec((1, tq, LANES), main),
                pl.BlockSpec((1, WINDOW, LANES), nxt)]
    return pl.pallas_call(
        functools.partial(_attn_a_kernel, tq=tq, seq=seq),
        grid=(bsz, seq // tq),
        in_specs=[pl.BlockSpec(memory_space=pltpu.SMEM), pl.BlockSpec((1, tq, A_WIDTH), main)]
                 + kv_specs + kv_specs + [pl.BlockSpec((1, tq, A_WIDTH), main)],
        out_specs=pl.BlockSpec((1, tq, A_WIDTH), main),
        out_shape=jax.ShapeDtypeStruct((bsz, seq, A_WIDTH), BF16),
        compiler_params=_cparams(("parallel", "parallel")),
    )(sink, q, k, k, k, v, v, v, z)


def _attn_b_kernel(lam_ref, q_ref, k_ref, v_ref, z_ref, g_ref, o_ref, acc_ref, m_ref, *, tq, tk, seq):
    q = q_ref[0]
    low = lax.broadcasted_iota(jnp.int32, (1, LANES), 1) < HEAD_DIM
    zero = jnp.zeros_like(q)
    q2 = jnp.concatenate([jnp.where(low, q, zero), jnp.where(low, zero, q)], axis=0)
    acc_ref[...] = jnp.zeros_like(acc_ref)
    m_ref[...] = jnp.full_like(m_ref, -jnp.inf)
    ones = jnp.ones((tk, LANES), BF16)

    def body(j, carry):
        start = pl.multiple_of(j * tk, tk)
        kc = k_ref[0, pl.ds(start, tk), :]
        vc = jnp.concatenate([v_ref[0, pl.ds(start, tk), :], ones], axis=1)
        s = lax.dot_general(q2, kc, (((1,), (1,)), ((), ())), preferred_element_type=F32)
        m_old = m_ref[...]
        m_new = jnp.maximum(m_old, jnp.max(s, axis=-1, keepdims=True))
        p = jnp.exp(s - m_new).astype(BF16)
        acc_ref[...] = acc_ref[...] * jnp.exp(m_old - m_new) + jnp.dot(p, vc, preferred_element_type=F32)
        m_ref[...] = m_new
        return carry

    lax.fori_loop(0, seq // tk, body, 0)
    acc = acc_ref[...]
    o1 = acc[:tq, :LANES] / acc[:tq, LANES:]
    o2 = acc[tq:, :LANES] / acc[tq:, LANES:]
    o = o1 - lam_ref[0] * o2
    o = o * lax.rsqrt(jnp.mean(o * o, axis=-1, keepdims=True) + EPS) * g_ref[...] * lam_ref[1]
    o_ref[0] = (o * z_ref[0].astype(F32)).astype(BF16)


def _attn_b(lam, q, k, v, z, g, tq, tk):
    bsz, seq, _ = q.shape
    qmap = lambda b, h, i: (b, i, h)
    kmap = lambda b, h, i: (b, 0, h)
    return pl.pallas_call(
        functools.partial(_attn_b_kernel, tq=tq, tk=tk, seq=seq),
        grid=(bsz, B_HEADS, seq // tq),
        in_specs=[pl.BlockSpec(memory_space=pltpu.SMEM),
                  pl.BlockSpec((1, tq, LANES), qmap),
                  pl.BlockSpec((1, seq, LANES), kmap),
                  pl.BlockSpec((1, seq, LANES), kmap),
                  pl.BlockSpec((1, tq, LANES), qmap),
                  pl.BlockSpec((1, LANES), lambda b, h, i: (0, 0))],
        out_specs=pl.BlockSpec((1, tq, LANES), qmap),
        out_shape=jax.ShapeDtypeStruct((bsz, seq, B_WIDTH), BF16),
        scratch_shapes=[pltpu.VMEM((2 * tq, 2 * LANES), F32), pltpu.VMEM((2 * tq, 1), F32)],
        compiler_params=_cparams(("parallel", "parallel", "parallel")),
    )(lam, q, k, v, z, g)


def _conv_kernel(xp_ref, x_ref, xn_ref, w_ref, o_ref, buf_ref, *, tc, seq):
    i = pl.program_id(0)
    pos = (i * tc) % seq
    halo = (C_CONV - 1) // 2
    buf_ref[0:SUBLANES, :] = jnp.where(pos == 0, 0.0, xp_ref[...])
    buf_ref[SUBLANES:SUBLANES + tc, :] = x_ref[...]
    buf_ref[SUBLANES + tc:, :] = jnp.where(pos + tc == seq, 0.0, xn_ref[...])
    w = w_ref[...]
    y = jnp.zeros((tc, 3 * C_WIDTH), F32)
    for t in range(C_CONV):
        y = y + buf_ref[SUBLANES - halo + t:SUBLANES - halo + t + tc, :] * w[t:t + 1, :]
    y = y * jax.nn.sigmoid(y)
    for j in range(3 * C_HEADS):
        blk = y[:, j * LANES:(j + 1) * LANES]
        if j < 2 * C_HEADS:
            blk = blk * lax.rsqrt(jnp.sum(blk * blk, axis=-1, keepdims=True) + EPS)
        if j < C_HEADS:
            blk = blk * (C_HEAD_DIM ** -0.5)
        o_ref[:, j * LANES:(j + 1) * LANES] = blk


def _conv(x2, w, seq, tc):
    rows, width = x2.shape
    r = tc // SUBLANES
    nb8 = rows // SUBLANES
    return pl.pallas_call(
        functools.partial(_conv_kernel, tc=tc, seq=seq),
        grid=(rows // tc,),
        in_specs=[pl.BlockSpec((SUBLANES, width), lambda i: (jnp.maximum(i * r - 1, 0), 0)),
                  pl.BlockSpec((tc, width), lambda i: (i, 0)),
                  pl.BlockSpec((SUBLANES, width), lambda i: (jnp.minimum((i + 1) * r, nb8 - 1), 0)),
                  pl.BlockSpec((C_CONV, width), lambda i: (0, 0))],
        out_specs=pl.BlockSpec((tc, width), lambda i: (i, 0)),
        out_shape=jax.ShapeDtypeStruct((rows, width), F32),
        scratch_shapes=[pltpu.VMEM((tc + 2 * SUBLANES, width), F32)],
        compiler_params=_cparams(("parallel",)),
    )(x2, x2, x2, w)


def _dot_hi(a, b):
    return jnp.dot(a, b, preferred_element_type=F32, precision=lax.Precision.HIGHEST)


def _unit_tri_inverse(a):
    n = a.shape[0]
    eye = (lax.broadcasted_iota(jnp.int32, (n, n), 0) == lax.broadcasted_iota(jnp.int32, (n, n), 1)).astype(F32)
    t = eye - a
    pw = _dot_hi(a, a)
    for _ in range(int(math.log2(n)) - 2):
        both = _dot_hi(jnp.concatenate([t, pw], axis=0), pw)
        t = t + both[:n]
        pw = both[n:]
    return t + _dot_hi(t, pw)


def _delta_kernel(qf_ref, kf_ref, vf_ref, bgf_ref, bgtf_ref, qb_ref, kb_ref, vb_ref, bgb_ref, bgtb_ref,
                  of_ref, ob_ref, state_ref):
    n = pl.program_id(1)

    @pl.when(n == 0)
    def _():
        state_ref[...] = jnp.zeros_like(state_ref)

    c = C_CHUNK
    ri = lax.broadcasted_iota(jnp.int32, (c, c), 0)
    ci = lax.broadcasted_iota(jnp.int32, (c, c), 1)
    dirs = ((qf_ref, kf_ref, vf_ref, bgf_ref, bgtf_ref, of_ref, ri >= ci, ri <= ci, ri > ci),
            (qb_ref, kb_ref, vb_ref, bgb_ref, bgtb_ref, ob_ref, ri <= ci, ri >= ci, ri < ci))
    for d, (q_ref, k_ref, v_ref, bg_ref, bgt_ref, o_ref, incl, incl_t, strict) in enumerate(dirs):
        bg = bg_ref[0]
        bgt = bgt_ref[0]
        for h in range(C_HEADS):
            sl = slice(h * LANES, (h + 1) * LANES)
            q = q_ref[0, :, sl]
            k = k_ref[0, :, sl]
            v = v_ref[0, :, sl]
            ib = d * C_HEADS + h
            ig = 2 * C_HEADS + ib
            beta = bg[:, ib:ib + 1]
            g_col = bg[:, ig:ig + 1]
            g_row = bgt[ig:ig + 1, :]
            gc_col = jnp.sum(jnp.where(incl, g_row, 0.0), axis=1, keepdims=True)
            gc_row = jnp.sum(jnp.where(incl_t, g_col, 0.0), axis=0, keepdims=True)
            g_tot = jnp.sum(g_row, axis=1, keepdims=True)
            decay = jnp.where(incl, jnp.exp(jnp.where(incl, gc_col - gc_row, 0.0)), 0.0)
            kbf = k.astype(BF16)
            qk = jnp.concatenate([kbf, q.astype(BF16)], axis=0)
            kkqk = lax.dot_general(qk, kbf, (((1,), (1,)), ((), ())), preferred_element_type=F32)
            amat = jnp.where(strict, beta * kkqk[:c] * decay, 0.0)
            attn = jnp.where(incl, kkqk[c:] * decay, 0.0)
            tmat = _unit_tri_inverse(amat)
            e_gc = jnp.exp(gc_col)
            rhs = jnp.concatenate([v * beta, k * (beta * e_gc)], axis=1)
            uw = jnp.dot(tmat.astype(BF16), rhs.astype(BF16), preferred_element_type=F32)
            u = uw[:, :LANES]
            w = uw[:, LANES:]
            state = state_ref[ib]
            sb = state.astype(BF16)
            lhs = jnp.concatenate([w, q * e_gc], axis=0).astype(BF16)
            ws_qs = jnp.dot(lhs, sb, preferred_element_type=F32)
            v_new = u - ws_qs[:c]
            vnb = v_new.astype(BF16)
            o = ws_qs[c:] + jnp.dot(attn.astype(BF16), vnb, preferred_element_type=F32)
            k_tail = (k * jnp.exp(g_tot - gc_col)).astype(BF16)
            state_ref[ib] = state * jnp.exp(g_tot) + lax.dot_general(
                k_tail, vnb, (((0,), (0,)), ((), ())), preferred_element_type=F32)
            o_ref[0, :, sl] = o


def _delta(qkv, bg, bgt):
    bsz, seq, _ = qkv.shape
    nch = seq // C_CHUNK
    c = C_CHUNK
    nh = C_WIDTH // LANES

    def fwd(off):
        return lambda b, n: (b, n, off)

    def bwd(off):
        return lambda b, n: (b, nch - 1 - n, off)

    def specs(mk):
        return [pl.BlockSpec((1, c, C_WIDTH), mk(0)), pl.BlockSpec((1, c, C_WIDTH), mk(1)),
                pl.BlockSpec((1, c, C_WIDTH), mk(2)), pl.BlockSpec((1, c, N_BA), mk(0)),
                pl.BlockSpec((1, N_BA, c), (lambda b, n: (b * nch + n, 0, 0)) if mk is fwd
                             else (lambda b, n: (b * nch + nch - 1 - n, 0, 0)))]

    out = jax.ShapeDtypeStruct((bsz, seq, C_WIDTH), F32)
    return pl.pallas_call(
        _delta_kernel,
        grid=(bsz, nch),
        in_specs=specs(fwd) + specs(bwd),
        out_specs=[pl.BlockSpec((1, c, C_WIDTH), fwd(0)), pl.BlockSpec((1, c, C_WIDTH), bwd(0))],
        out_shape=[out, out],
        scratch_shapes=[pltpu.VMEM((2 * C_HEADS, C_HEAD_DIM, C_HEAD_DIM), F32)],
        compiler_params=_cparams(("parallel", "arbitrary")),
    )(qkv, qkv, qkv, bg, bgt, qkv, qkv, qkv, bg, bgt)


def _merge_kernel(x_ref, oa_ref, ob_ref, ocf_ref, ocb_ref, zc_ref, gates_ref, cg_ref, wa_ref, wb_ref, wc_ref,
                  wo_ref, fg_ref, o_ref, *, final):
    oc = ocf_ref[...] + ocb_ref[...]
    cg = cg_ref[...]
    parts = []
    for h in range(C_HEADS):
        blk = oc[:, h * LANES:(h + 1) * LANES]
        blk = blk * lax.rsqrt(jnp.mean(blk * blk, axis=-1, keepdims=True) + EPS) * cg
        parts.append((blk * zc_ref[:, h * LANES:(h + 1) * LANES].astype(F32)).astype(BF16))
    ocn = jnp.concatenate(parts, axis=1)
    ya = jnp.dot(oa_ref[...], wa_ref[...], preferred_element_type=F32)
    yb = jnp.dot(ob_ref[...], wb_ref[...], preferred_element_type=F32)
    yc = jnp.dot(ocn, wc_ref[...], preferred_element_type=F32)
    d = D_MODEL
    merged = (gates_ref[:, 0:d].astype(F32) * ya + gates_ref[:, d:2 * d].astype(F32) * yb
              + gates_ref[:, 2 * d:3 * d].astype(F32) * yc)
    x = x_ref[...] + jnp.dot(merged.astype(BF16), wo_ref[...], preferred_element_type=F32)
    if final:
        x = x * lax.rsqrt(jnp.mean(x * x, axis=-1, keepdims=True) + EPS) * fg_ref[...]
    o_ref[...] = x


def _merge(x2, oa, ob, ocf, ocb, zc, gates, cg, wa, wb, wc, wo, fg, tm, final):
    rows = x2.shape[0]
    row = lambda i: (i, 0)
    const = lambda i: (0, 0)
    return pl.pallas_call(
        functools.partial(_merge_kernel, final=final),
        grid=(rows // tm,),
        in_specs=[pl.BlockSpec((tm, D_MODEL), row),
                  pl.BlockSpec((tm, A_WIDTH), row), pl.BlockSpec((tm, B_WIDTH), row),
                  pl.BlockSpec((tm, C_WIDTH), row), pl.BlockSpec((tm, C_WIDTH), row),
                  pl.BlockSpec((tm, C_WIDTH), row), pl.BlockSpec((tm, N_BRANCH * D_MODEL), row),
                  pl.BlockSpec((1, LANES), const),
                  pl.BlockSpec((A_WIDTH, D_MODEL), const), pl.BlockSpec((B_WIDTH, D_MODEL), const),
                  pl.BlockSpec((C_WIDTH, D_MODEL), const), pl.BlockSpec((D_MODEL, D_MODEL), const),
                  pl.BlockSpec((1, D_MODEL), const)],
        out_specs=pl.BlockSpec((tm, D_MODEL), row),
        out_shape=jax.ShapeDtypeStruct((rows, D_MODEL), F32),
        compiler_params=_cparams(("parallel",)),
    )(x2, oa, ob, ocf, ocb, zc, gates, cg, wa, wb, wc, wo, fg)


def _rope_tables(seq):
    inv = 1.0 / (ROPE_THETA ** (jnp.arange(0, HEAD_DIM, 2, dtype=F32) / HEAD_DIM))
    ang = jnp.arange(seq, dtype=F32)[:, None] * inv[None, :]
    cos = jnp.tile(jnp.cos(ang), (1, LANES // (HEAD_DIM // 2)))
    sin = jnp.tile(jnp.concatenate([-jnp.sin(ang), jnp.sin(ang)], axis=1), (1, LANES // HEAD_DIM))
    return cos, sin


def kernel(x, norm_g, w_in, a_sink, b_lambda, b_subln_g, c_conv_w, c_a_log, c_dt_bias, c_norm_g,
           w_bo_a, w_bo_b, w_bo_c, w_out, final_g):
    bsz, seq, _ = x.shape
    depth = w_in.shape[0]
    rows = bsz * seq
    tm = min(256, seq)
    tq_a = min(256, seq)
    tq_b = min(256, seq)
    tk_b = min(512, seq)
    cos, sin = _rope_tables(seq)

    scale = np.ones((len(_COL_IDX),), np.float32)
    for name in ('aq', 'bq'):
        lo, n = _SEC[name]
        scale[lo:lo + n] = HEAD_DIM ** -0.5
    w_all = jnp.take(w_in, jnp.asarray(_COL_IDX), axis=2) * jnp.asarray(scale)
    w_main = w_all[:, :, :N_PROJ].astype(BF16)
    w_ba = w_all[:, :, N_PROJ:].astype(BF16)
    w_bat = jnp.swapaxes(w_ba, 1, 2)
    dec = jnp.stack([jnp.concatenate([jnp.zeros((depth, 2 * C_HEADS), F32),
                                      -jnp.exp(c_a_log.astype(F32)).reshape(depth, 2 * C_HEADS)], axis=1),
                     jnp.concatenate([jnp.zeros((depth, 2 * C_HEADS), F32),
                                      c_dt_bias.astype(F32).reshape(depth, 2 * C_HEADS)], axis=1)], axis=1)
    dect = jnp.swapaxes(dec, 1, 2)
    wa = jnp.take(w_bo_a, jnp.asarray(_PERM_A), axis=1).astype(BF16)
    wb = w_bo_b.astype(BF16)
    wc = w_bo_c.astype(BF16)
    wo = w_out.astype(BF16)

    x2 = x.reshape(rows, D_MODEL)
    for l in range(depth):
        (qa, ka, va, za, qb, kb, vb, zb, cqkv, zc, gates, bg, bgt) = _inproj(
            x2, norm_g[l][None], w_main[l], w_ba[l], w_bat[l], cos, sin, dec[l], dect[l], seq, tm)

        r3 = lambda t: t.reshape(bsz, seq, t.shape[-1])
        oa = _attn_a(a_sink[l].astype(F32), r3(qa), r3(ka), r3(va), r3(za), tq_a)

        lam_init = 0.8 - 0.6 * math.exp(-0.3 * l)
        bl = b_lambda[l].astype(F32)
        lam = jnp.exp(jnp.sum(bl[0] * bl[1])) - jnp.exp(jnp.sum(bl[2] * bl[3])) + lam_init
        lam_s = jnp.stack([lam, jnp.asarray(1.0 - lam_init, F32)])
        ob = _attn_b(lam_s, r3(qb), r3(kb), r3(vb), r3(zb), b_subln_g[l][None].astype(F32), tq_b, tk_b)

        qkv = _conv(cqkv, c_conv_w[l].astype(F32), seq, min(256, seq))
        bgt_c = bgt.reshape(N_BA, rows // C_CHUNK, C_CHUNK).transpose(1, 0, 2)
        ocf, ocb = _delta(r3(qkv), r3(bg), bgt_c)

        x2 = _merge(x2, oa.reshape(rows, A_WIDTH), ob.reshape(rows, B_WIDTH),
                    ocf.reshape(rows, C_WIDTH), ocb.reshape(rows, C_WIDTH), zc, gates,
                    c_norm_g[l][None].astype(F32), wa[l], wb[l], wc[l], wo[l],
                    final_g[None].astype(F32), tm, l == depth - 1)
    return x2.reshape(bsz, seq, D_MODEL)
```

```python
import functools
import math

import numpy as np
import jax
import jax.numpy as jnp
from jax import lax
from jax.experimental import pallas as pl
from jax.experimental.pallas import tpu as pltpu

F32 = jnp.float32
BF16 = jnp.bfloat16

D_MODEL = 1024
HEAD_DIM = 64
ROPE_THETA = 10000.0
EPS = 1e-6
A_Q_HEADS = 8
A_KV_HEADS = 2
A_WIDTH = A_Q_HEADS * HEAD_DIM
A_KV_WIDTH = A_KV_HEADS * HEAD_DIM
WINDOW = 128
B_HEADS = 4
B_V_DIM = 2 * HEAD_DIM
B_WIDTH = B_HEADS * B_V_DIM
C_HEADS = 4
C_HEAD_DIM = 128
C_WIDTH = C_HEADS * C_HEAD_DIM
C_CONV = 5
C_CHUNK = 64
N_BRANCH = 3
LOG2E = math.log2(math.e)
LANES = 128
SUBLANES = 8
VMEM_LIMIT = 56 * 1024 * 1024

IN_SPLITS = (A_WIDTH, A_KV_WIDTH, A_KV_WIDTH, A_WIDTH,
             B_WIDTH, B_WIDTH, B_WIDTH, B_WIDTH,
             3 * C_WIDTH, C_WIDTH, 2 * C_HEADS, 2 * C_HEADS,
             N_BRANCH * D_MODEL)
_OFF = [0] + [int(o) for o in np.cumsum(IN_SPLITS)]

_PERM_A = np.concatenate([np.concatenate([np.arange(64 * b, 64 * b + 64),
                                          np.arange(64 * (4 + b), 64 * (4 + b) + 64)])
                          for b in range(4)])
_SEC = {}
_cols = []


def _add_section(name, idx):
    _SEC[name] = (sum(len(c) for c in _cols), len(idx))
    _cols.append(np.asarray(idx))


_add_section('aq', _OFF[0] + _PERM_A)
_add_section('ak', np.arange(_OFF[1], _OFF[2]))
_add_section('av', np.arange(_OFF[2], _OFF[3]))
_add_section('az', _OFF[3] + _PERM_A)
_add_section('bq', np.arange(_OFF[4], _OFF[5]))
_add_section('bk', np.arange(_OFF[5], _OFF[6]))
_add_section('bv', np.arange(_OFF[6], _OFF[7]))
_add_section('bz', np.arange(_OFF[7], _OFF[8]))
_add_section('cqkv', np.arange(_OFF[8], _OFF[9]))
_add_section('cz', np.arange(_OFF[9], _OFF[10]))
_add_section('gates', np.arange(_OFF[12], _OFF[13]))
_add_section('cba', np.arange(_OFF[10], _OFF[12]))
_COL_IDX = np.concatenate(_cols)
N_BA = 4 * C_HEADS
N_PROJ = len(_COL_IDX) - N_BA


def _cparams(sem):
    return pltpu.CompilerParams(dimension_semantics=sem, vmem_limit_bytes=VMEM_LIMIT)


def _rope(t, cos, sin_signed, first_half):
    swapped = jnp.where(first_half, pltpu.roll(t, LANES - 32, 1), pltpu.roll(t, 32, 1))
    return t * cos + swapped * sin_signed


def _softplus(t):
    return jnp.maximum(t, 0.0) + jnp.log(1.0 + jnp.exp(-jnp.abs(t)))


def _inproj_kernel(x_ref, g_ref, w_ref, wba_ref, wbat_ref, cos_ref, sin_ref, dec_ref, dect_ref,
                   qa_ref, ka_ref, va_ref, za_ref, qb_ref, kb_ref, vb_ref, zb_ref,
                   cqkv_ref, zc_ref, gates_ref, bg_ref, bgt_ref):
    x = x_ref[...]
    h = x * lax.rsqrt(jnp.mean(x * x, axis=-1, keepdims=True) + EPS) * g_ref[...]
    hb = h.astype(BF16)
    cos = cos_ref[...]
    sin = sin_ref[...]
    first_half = (lax.broadcasted_iota(jnp.int32, cos.shape, 1) & 32) == 0

    def proj(name, blk):
        lo = _SEC[name][0] + blk * LANES
        return jnp.dot(hb, w_ref[:, lo:lo + LANES], preferred_element_type=F32)

    def nblk(name):
        return _SEC[name][1] // LANES

    for name, ref, mul in (('aq', qa_ref, 1.0), ('ak', ka_ref, 1.0), ('bq', qb_ref, LOG2E), ('bk', kb_ref, 1.0)):
        for j in range(nblk(name)):
            t = _rope(proj(name, j), cos, sin, first_half)
            ref[:, j * LANES:(j + 1) * LANES] = (t if mul == 1.0 else t * mul).astype(BF16)
    for name, ref in (('av', va_ref), ('bv', vb_ref)):
        for j in range(nblk(name)):
            ref[:, j * LANES:(j + 1) * LANES] = proj(name, j).astype(BF16)
    for name, ref in (('az', za_ref), ('bz', zb_ref), ('cz', zc_ref)):
        for j in range(nblk(name)):
            t = proj(name, j)
            ref[:, j * LANES:(j + 1) * LANES] = (t * jax.nn.sigmoid(t)).astype(BF16)
    for j in range(nblk('gates')):
        gates_ref[:, j * LANES:(j + 1) * LANES] = jax.nn.sigmoid(proj('gates', j)).astype(BF16)
    for j in range(nblk('cqkv')):
        cqkv_ref[:, j * LANES:(j + 1) * LANES] = proj('cqkv', j)

    ba = jnp.dot(hb, wba_ref[...], preferred_element_type=F32)
    bat = lax.dot_general(wbat_ref[...], hb, (((1,), (1,)), ((), ())),
                          preferred_element_type=F32)
    is_beta = lax.broadcasted_iota(jnp.int32, ba.shape, 1) < 2 * C_HEADS
    dec = dec_ref[...]
    bg_ref[...] = jnp.where(is_beta, jax.nn.sigmoid(ba), dec[0:1, :] * _softplus(ba + dec[1:2, :]))
    is_beta_t = lax.broadcasted_iota(jnp.int32, bat.shape, 0) < 2 * C_HEADS
    dect = dect_ref[...]
    bgt_ref[...] = jnp.where(is_beta_t, jax.nn.sigmoid(bat),
                             dect[:, 0:1] * _softplus(bat + dect[:, 1:2]))


def _inproj(x2, g, w, wba, wbat, cos, sin, dec, dect, seq, tm):
    rows = x2.shape[0]
    nseq = seq // tm
    row = lambda i: (i, 0)
    const = lambda i: (0, 0)
    single = pl.Buffered(1)
    out_widths = [('aq', BF16), ('ak', BF16), ('av', BF16), ('az', BF16), ('bq', BF16), ('bk', BF16),
                  ('bv', BF16), ('bz', BF16), ('cqkv', F32), ('cz', BF16), ('gates', BF16)]
    out_shape = [jax.ShapeDtypeStruct((rows, _SEC[n][1]), dt) for n, dt in out_widths]
    out_specs = [pl.BlockSpec((tm, _SEC[n][1]), row) for n, _ in out_widths]
    out_shape += [jax.ShapeDtypeStruct((rows, N_BA), F32), jax.ShapeDtypeStruct((N_BA, rows), F32)]
    out_specs += [pl.BlockSpec((tm, N_BA), row), pl.BlockSpec((N_BA, tm), lambda i: (0, i))]
    return pl.pallas_call(
        _inproj_kernel,
        grid=(rows // tm,),
        in_specs=[pl.BlockSpec((tm, D_MODEL), row),
                  pl.BlockSpec((1, D_MODEL), const),
                  pl.BlockSpec((D_MODEL, N_PROJ), const, pipeline_mode=single),
                  pl.BlockSpec((D_MODEL, N_BA), const),
                  pl.BlockSpec((N_BA, D_MODEL), const),
                  pl.BlockSpec((tm, LANES), lambda i: (i % nseq, 0)),
                  pl.BlockSpec((tm, LANES), lambda i: (i % nseq, 0)),
                  pl.BlockSpec((2, N_BA), const),
                  pl.BlockSpec((N_BA, 2), const)],
        out_specs=out_specs,
        out_shape=out_shape,
        compiler_params=_cparams(("parallel",)),
    )(x2, g, w, wba, wbat, cos, sin, dec, dect)


def _attn_a_kernel(sink_ref, q_ref, kp_ref, k_ref, kn_ref, vp_ref, v_ref, vn_ref, z_ref, o_ref, *, tq, seq):
    i = pl.program_id(1)
    kcat = jnp.concatenate([kp_ref[0], k_ref[0], kn_ref[0]], axis=0)
    vcat = jnp.concatenate([vp_ref[0], v_ref[0], vn_ref[0]], axis=0)
    nk = tq + 2 * WINDOW
    lane = lax.broadcasted_iota(jnp.int32, (1, LANES), 1)
    low = lane < HEAD_DIM
    v_lo = jnp.where(low, vcat, jnp.zeros_like(vcat))
    v_hi = jnp.where(low, jnp.zeros_like(vcat), vcat)
    r = lax.broadcasted_iota(jnp.int32, (tq, nk), 0)
    c = lax.broadcasted_iota(jnp.int32, (tq, nk), 1)
    kabs = i * tq - WINDOW + c
    valid = (jnp.abs(c - WINDOW - r) <= WINDOW) & (kabs >= 0) & (kabs < seq)

    def probs(qm, sink):
        s = lax.dot_general(qm, kcat, (((1,), (1,)), ((), ())), preferred_element_type=F32)
        s = jnp.where(valid, s, -1e30)
        m = jnp.maximum(jnp.max(s, axis=-1, keepdims=True), sink)
        p = jnp.exp(s - m)
        l = jnp.sum(p, axis=-1, keepdims=True) + jnp.exp(sink - m)
        return (p / l).astype(BF16)

    for b in range(A_Q_HEADS // 2):
        q = q_ref[0, :, b * LANES:(b + 1) * LANES]
        p0 = probs(jnp.where(low, q, jnp.zeros_like(q)), sink_ref[b])
        p1 = probs(jnp.where(low, jnp.zeros_like(q), q), sink_ref[A_Q_HEADS // 2 + b])
        o = (jnp.dot(p0, v_lo, preferred_element_type=F32)
             + jnp.dot(p1, v_hi, preferred_element_type=F32))
        o_ref[0, :, b * LANES:(b + 1) * LANES] = (
            o * z_ref[0, :, b * LANES:(b + 1) * LANES].astype(F32)).astype(BF16)


def _attn_a(sink, q, k, v, z, tq):
    bsz, seq, _ = q.shape
    r = tq // WINDOW
    nwb = seq // WINDOW
    main = lambda b, i: (b, i, 0)
    prev = lambda b, i: (b, jnp.maximum(i * r - 1, 0), 0)
    nxt = lambda b, i: (b, jnp.minimum((i + 1) * r, nwb - 1), 0)
    kv_specs = [pl.BlockSpec((1, WINDOW, LANES), prev), pl.BlockSpec((1, tq, LANES), main),
                pl.BlockSpec((1, WINDOW, LANES), nxt)]
    return pl.pallas_call(
        functools.partial(_attn_a_kernel, tq=tq, seq=seq),
        grid=(bsz, seq // tq),
        in_specs=[pl.BlockSpec(memory_space=pltpu.SMEM), pl.BlockSpec((1, tq, A_WIDTH), main)]
                 + kv_specs + kv_specs + [pl.BlockSpec((1, tq, A_WIDTH), main)],
        out_specs=pl.BlockSpec((1, tq, A_WIDTH), main),
        out_shape=jax.ShapeDtypeStruct((bsz, seq, A_WIDTH), BF16),
        compiler_params=_cparams(("parallel", "parallel")),
    )(sink, q, k, k, k, v, v, v, z)


def _attn_b_kernel(lam_ref, q_ref, k_ref, v_ref, z_ref, g_ref, o_ref, acc_ref, m_ref, sa_ref, sb_ref,
                   *, tq, tk, seq):
    q = q_ref[0]
    low = lax.broadcasted_iota(jnp.int32, (1, LANES), 1) < HEAD_DIM
    zero = jnp.zeros_like(q)
    q2 = jnp.concatenate([jnp.where(low, q, zero), jnp.where(low, zero, q)], axis=0)
    acc_ref[...] = jnp.zeros_like(acc_ref)
    m_ref[...] = jnp.full_like(m_ref, -jnp.inf)
    ones = jnp.ones((tk, LANES), BF16)
    nk = seq // tk

    def scores(j):
        start = pl.multiple_of(j * tk, tk)
        return lax.dot_general(q2, k_ref[0, pl.ds(start, tk), :], (((1,), (1,)), ((), ())),
                               preferred_element_type=F32)

    def update(s_ref, j):
        start = pl.multiple_of(j * tk, tk)
        vc = jnp.concatenate([v_ref[0, pl.ds(start, tk), :], ones], axis=1)
        s = s_ref[...]
        m_old = m_ref[...]
        m_new = jnp.maximum(m_old, jnp.max(s, axis=-1, keepdims=True))
        p = jnp.exp2(s - m_new).astype(BF16)
        acc_ref[...] = acc_ref[...] * jnp.exp2(m_old - m_new) + jnp.dot(p, vc, preferred_element_type=F32)
        m_ref[...] = m_new

    sa_ref[...] = scores(0)

    def body(jj, carry):
        j = 2 * jj
        sb_ref[...] = scores(j + 1)
        update(sa_ref, j)
        sa_ref[...] = scores(jnp.minimum(j + 2, nk - 1))
        update(sb_ref, j + 1)
        return carry

    lax.fori_loop(0, nk // 2, body, 0)
    acc = acc_ref[...]
    o1 = acc[:tq, :LANES] / acc[:tq, LANES:]
    o2 = acc[tq:, :LANES] / acc[tq:, LANES:]
    o = o1 - lam_ref[0] * o2
    o = o * lax.rsqrt(jnp.mean(o * o, axis=-1, keepdims=True) + EPS) * g_ref[...] * lam_ref[1]
    o_ref[0] = (o * z_ref[0].astype(F32)).astype(BF16)


def _attn_b(lam, q, k, v, z, g, tq, tk):
    bsz, seq, _ = q.shape
    qmap = lambda b, h, i: (b, i, h)
    kmap = lambda b, h, i: (b, 0, h)
    return pl.pallas_call(
        functools.partial(_attn_b_kernel, tq=tq, tk=tk, seq=seq),
        grid=(bsz, B_HEADS, seq // tq),
        in_specs=[pl.BlockSpec(memory_space=pltpu.SMEM),
                  pl.BlockSpec((1, tq, LANES), qmap),
                  pl.BlockSpec((1, seq, LANES), kmap),
                  pl.BlockSpec((1, seq, LANES), kmap),
                  pl.BlockSpec((1, tq, LANES), qmap),
                  pl.BlockSpec((1, LANES), lambda b, h, i: (0, 0))],
        out_specs=pl.BlockSpec((1, tq, LANES), qmap),
        out_shape=jax.ShapeDtypeStruct((bsz, seq, B_WIDTH), BF16),
        scratch_shapes=[pltpu.VMEM((2 * tq, 2 * LANES), F32), pltpu.VMEM((2 * tq, 1), F32),
                        pltpu.VMEM((2 * tq, tk), F32), pltpu.VMEM((2 * tq, tk), F32)],
        compiler_params=_cparams(("parallel", "parallel", "parallel")),
    )(lam, q, k, v, z, g)


def _conv_kernel(xp_ref, x_ref, xn_ref, w_ref, o_ref, buf_ref, *, tc, seq):
    i = pl.program_id(0)
    pos = (i * tc) % seq
    halo = (C_CONV - 1) // 2
    buf_ref[0:SUBLANES, :] = jnp.where(pos == 0, 0.0, xp_ref[...])
    buf_ref[SUBLANES:SUBLANES + tc, :] = x_ref[...]
    buf_ref[SUBLANES + tc:, :] = jnp.where(pos + tc == seq, 0.0, xn_ref[...])
    w = w_ref[...]
    y = jnp.zeros((tc, 3 * C_WIDTH), F32)
    for t in range(C_CONV):
        y = y + buf_ref[SUBLANES - halo + t:SUBLANES - halo + t + tc, :] * w[t:t + 1, :]
    y = y * jax.nn.sigmoid(y)
    for j in range(3 * C_HEADS):
        blk = y[:, j * LANES:(j + 1) * LANES]
        if j < 2 * C_HEADS:
            blk = blk * lax.rsqrt(jnp.sum(blk * blk, axis=-1, keepdims=True) + EPS)
        if j < C_HEADS:
            blk = blk * (C_HEAD_DIM ** -0.5)
        o_ref[:, j * LANES:(j + 1) * LANES] = blk


def _conv(x2, w, seq, tc):
    rows, width = x2.shape
    r = tc // SUBLANES
    nb8 = rows // SUBLANES
    return pl.pallas_call(
        functools.partial(_conv_kernel, tc=tc, seq=seq),
        grid=(rows // tc,),
        in_specs=[pl.BlockSpec((SUBLANES, width), lambda i: (jnp.maximum(i * r - 1, 0), 0)),
                  pl.BlockSpec((tc, width), lambda i: (i, 0)),
                  pl.BlockSpec((SUBLANES, width), lambda i: (jnp.minimum((i + 1) * r, nb8 - 1), 0)),
                  pl.BlockSpec((C_CONV, width), lambda i: (0, 0))],
        out_specs=pl.BlockSpec((tc, width), lambda i: (i, 0)),
        out_shape=jax.ShapeDtypeStruct((rows, width), F32),
        scratch_shapes=[pltpu.VMEM((tc + 2 * SUBLANES, width), F32)],
        compiler_params=_cparams(("parallel",)),
    )(x2, x2, x2, w)


def _dot_hi(a, b):
    return jnp.dot(a, b, preferred_element_type=F32, precision=lax.Precision.HIGHEST)


def _unit_tri_inverse(a):
    n = a.shape[0]
    eye = (lax.broadcasted_iota(jnp.int32, (n, n), 0) == lax.broadcasted_iota(jnp.int32, (n, n), 1)).astype(F32)
    t = eye - a
    pw = _dot_hi(a, a)
    for _ in range(int(math.log2(n)) - 2):
        both = _dot_hi(jnp.concatenate([t, pw], axis=0), pw)
        t = t + both[:n]
        pw = both[n:]
    return t + _dot_hi(t, pw)


def _delta_kernel(qf_ref, kf_ref, vf_ref, bgf_ref, bgtf_ref, qb_ref, kb_ref, vb_ref, bgb_ref, bgtb_ref,
                  of_ref, ob_ref, state_ref):
    n = pl.program_id(1)

    @pl.when(n == 0)
    def _():
        state_ref[...] = jnp.zeros_like(state_ref)

    c = C_CHUNK
    ri = lax.broadcasted_iota(jnp.int32, (c, c), 0)
    ci = lax.broadcasted_iota(jnp.int32, (c, c), 1)
    dirs = ((qf_ref, kf_ref, vf_ref, bgf_ref, bgtf_ref, of_ref, ri >= ci, ri <= ci, ri > ci),
            (qb_ref, kb_ref, vb_ref, bgb_ref, bgtb_ref, ob_ref, ri <= ci, ri >= ci, ri < ci))
    for d, (q_ref, k_ref, v_ref, bg_ref, bgt_ref, o_ref, incl, incl_t, strict) in enumerate(dirs):
        bg = bg_ref[0]
        bgt = bgt_ref[0]
        for h in range(C_HEADS):
            sl = slice(h * LANES, (h + 1) * LANES)
            q = q_ref[0, :, sl]
            k = k_ref[0, :, sl]
            v = v_ref[0, :, sl]
            ib = d * C_HEADS + h
            ig = 2 * C_HEADS + ib
            beta = bg[:, ib:ib + 1]
            g_col = bg[:, ig:ig + 1]
            g_row = bgt[ig:ig + 1, :]
            gc_col = jnp.sum(jnp.where(incl, g_row, 0.0), axis=1, keepdims=True)
            gc_row = jnp.sum(jnp.where(incl_t, g_col, 0.0), axis=0, keepdims=True)
            g_tot = jnp.sum(g_row, axis=1, keepdims=True)
            decay = jnp.where(incl, jnp.exp(jnp.where(incl, gc_col - gc_row, 0.0)), 0.0)
            kbf = k.astype(BF16)
            qk = jnp.concatenate([kbf, q.astype(BF16)], axis=0)
            kkqk = lax.dot_general(qk, kbf, (((1,), (1,)), ((), ())), preferred_element_type=F32)
            amat = jnp.where(strict, beta * kkqk[:c] * decay, 0.0)
            attn = jnp.where(incl, kkqk[c:] * decay, 0.0)
            tmat = _unit_tri_inverse(amat)
            e_gc = jnp.exp(gc_col)
            rhs = jnp.concatenate([v * beta, k * (beta * e_gc)], axis=1)
            uw = jnp.dot(tmat.astype(BF16), rhs.astype(BF16), preferred_element_type=F32)
            u = uw[:, :LANES]
            w = uw[:, LANES:]
            state = state_ref[ib]
            sb = state.astype(BF16)
            lhs = jnp.concatenate([w, q * e_gc], axis=0).astype(BF16)
            ws_qs = jnp.dot(lhs, sb, preferred_element_type=F32)
            v_new = u - ws_qs[:c]
            vnb = v_new.astype(BF16)
            o = ws_qs[c:] + jnp.dot(attn.astype(BF16), vnb, preferred_element_type=F32)
            k_tail = (k * jnp.exp(g_tot - gc_col)).astype(BF16)
            state_ref[ib] = state * jnp.exp(g_tot) + lax.dot_general(
                k_tail, vnb, (((0,), (0,)), ((), ())), preferred_element_type=F32)
            o_ref[0, :, sl] = o


def _delta(qkv, bg, bgt):
    bsz, seq, _ = qkv.shape
    nch = seq // C_CHUNK
    c = C_CHUNK
    nh = C_WIDTH // LANES

    def fwd(off):
        return lambda b, n: (b, n, off)

    def bwd(off):
        return lambda b, n: (b, nch - 1 - n, off)

    def specs(mk):
        return [pl.BlockSpec((1, c, C_WIDTH), mk(0)), pl.BlockSpec((1, c, C_WIDTH), mk(1)),
                pl.BlockSpec((1, c, C_WIDTH), mk(2)), pl.BlockSpec((1, c, N_BA), mk(0)),
                pl.BlockSpec((1, N_BA, c), (lambda b, n: (b * nch + n, 0, 0)) if mk is fwd
                             else (lambda b, n: (b * nch + nch - 1 - n, 0, 0)))]

    out = jax.ShapeDtypeStruct((bsz, seq, C_WIDTH), F32)
    return pl.pallas_call(
        _delta_kernel,
        grid=(bsz, nch),
        in_specs=specs(fwd) + specs(bwd),
        out_specs=[pl.BlockSpec((1, c, C_WIDTH), fwd(0)), pl.BlockSpec((1, c, C_WIDTH), bwd(0))],
        out_shape=[out, out],
        scratch_shapes=[pltpu.VMEM((2 * C_HEADS, C_HEAD_DIM, C_HEAD_DIM), F32)],
        compiler_params=_cparams(("parallel", "arbitrary")),
    )(qkv, qkv, qkv, bg, bgt, qkv, qkv, qkv, bg, bgt)


def _merge_kernel(x_ref, oa_ref, ob_ref, ocf_ref, ocb_ref, zc_ref, gates_ref, cg_ref, wa_ref, wb_ref, wc_ref,
                  wo_ref, fg_ref, o_ref, *, final):
    oc = ocf_ref[...] + ocb_ref[...]
    cg = cg_ref[...]
    parts = []
    for h in range(C_HEADS):
        blk = oc[:, h * LANES:(h + 1) * LANES]
        blk = blk * lax.rsqrt(jnp.mean(blk * blk, axis=-1, keepdims=True) + EPS) * cg
        parts.append((blk * zc_ref[:, h * LANES:(h + 1) * LANES].astype(F32)).astype(BF16))
    ocn = jnp.concatenate(parts, axis=1)
    ya = jnp.dot(oa_ref[...], wa_ref[...], preferred_element_type=F32)
    yb = jnp.dot(ob_ref[...], wb_ref[...], preferred_element_type=F32)
    yc = jnp.dot(ocn, wc_ref[...], preferred_element_type=F32)
    d = D_MODEL
    merged = (gates_ref[:, 0:d].astype(F32) * ya + gates_ref[:, d:2 * d].astype(F32) * yb
              + gates_ref[:, 2 * d:3 * d].astype(F32) * yc)
    x = x_ref[...] + jnp.dot(merged.astype(BF16), wo_ref[...], preferred_element_type=F32)
    if final:
        x = x * lax.rsqrt(jnp.mean(x * x, axis=-1, keepdims=True) + EPS) * fg_ref[...]
    o_ref[...] = x


def _merge(x2, oa, ob, ocf, ocb, zc, gates, cg, wa, wb, wc, wo, fg, tm, final):
    rows = x2.shape[0]
    row = lambda i: (i, 0)
    const = lambda i: (0, 0)
    return pl.pallas_call(
        functools.partial(_merge_kernel, final=final),
        grid=(rows // tm,),
        in_specs=[pl.BlockSpec((tm, D_MODEL), row),
                  pl.BlockSpec((tm, A_WIDTH), row), pl.BlockSpec((tm, B_WIDTH), row),
                  pl.BlockSpec((tm, C_WIDTH), row), pl.BlockSpec((tm, C_WIDTH), row),
                  pl.BlockSpec((tm, C_WIDTH), row), pl.BlockSpec((tm, N_BRANCH * D_MODEL), row),
                  pl.BlockSpec((1, LANES), const),
                  pl.BlockSpec((A_WIDTH, D_MODEL), const), pl.BlockSpec((B_WIDTH, D_MODEL), const),
                  pl.BlockSpec((C_WIDTH, D_MODEL), const), pl.BlockSpec((D_MODEL, D_MODEL), const),
                  pl.BlockSpec((1, D_MODEL), const)],
        out_specs=pl.BlockSpec((tm, D_MODEL), row),
        out_shape=jax.ShapeDtypeStruct((rows, D_MODEL), F32),
        compiler_params=_cparams(("parallel",)),
    )(x2, oa, ob, ocf, ocb, zc, gates, cg, wa, wb, wc, wo, fg)


def _rope_tables(seq):
    inv = 1.0 / (ROPE_THETA ** (jnp.arange(0, HEAD_DIM, 2, dtype=F32) / HEAD_DIM))
    ang = jnp.arange(seq, dtype=F32)[:, None] * inv[None, :]
    cos = jnp.tile(jnp.cos(ang), (1, LANES // (HEAD_DIM // 2)))
    sin = jnp.tile(jnp.concatenate([-jnp.sin(ang), jnp.sin(ang)], axis=1), (1, LANES // HEAD_DIM))
    return cos, sin


def kernel(x, norm_g, w_in, a_sink, b_lambda, b_subln_g, c_conv_w, c_a_log, c_dt_bias, c_norm_g,
           w_bo_a, w_bo_b, w_bo_c, w_out, final_g):
    bsz, seq, _ = x.shape
    depth = w_in.shape[0]
    rows = bsz * seq
    tm = min(256, seq)
    tq_a = min(256, seq)
    tq_b = min(512, seq)
    tk_b = min(1024, seq)
    cos, sin = _rope_tables(seq)

    scale = np.ones((len(_COL_IDX),), np.float32)
    for name in ('aq', 'bq'):
        lo, n = _SEC[name]
        scale[lo:lo + n] = HEAD_DIM ** -0.5
    w_all = jnp.take(w_in, jnp.asarray(_COL_IDX), axis=2) * jnp.asarray(scale)
    w_main = w_all[:, :, :N_PROJ].astype(BF16)
    w_ba = w_all[:, :, N_PROJ:].astype(BF16)
    w_bat = jnp.swapaxes(w_ba, 1, 2)
    dec = jnp.stack([jnp.concatenate([jnp.zeros((depth, 2 * C_HEADS), F32),
                                      -jnp.exp(c_a_log.astype(F32)).reshape(depth, 2 * C_HEADS)], axis=1),
                     jnp.concatenate([jnp.zeros((depth, 2 * C_HEADS), F32),
                                      c_dt_bias.astype(F32).reshape(depth, 2 * C_HEADS)], axis=1)], axis=1)
    dect = jnp.swapaxes(dec, 1, 2)
    wa = jnp.take(w_bo_a, jnp.asarray(_PERM_A), axis=1).astype(BF16)
    wb = w_bo_b.astype(BF16)
    wc = w_bo_c.astype(BF16)
    wo = w_out.astype(BF16)

    x2 = x.reshape(rows, D_MODEL)
    for l in range(depth):
        (qa, ka, va, za, qb, kb, vb, zb, cqkv, zc, gates, bg, bgt) = _inproj(
            x2, norm_g[l][None], w_main[l], w_ba[l], w_bat[l], cos, sin, dec[l], dect[l], seq, tm)

        r3 = lambda t: t.reshape(bsz, seq, t.shape[-1])
        oa = _attn_a(a_sink[l].astype(F32), r3(qa), r3(ka), r3(va), r3(za), tq_a)

        lam_init = 0.8 - 0.6 * math.exp(-0.3 * l)
        bl = b_lambda[l].astype(F32)
        lam = jnp.exp(jnp.sum(bl[0] * bl[1])) - jnp.exp(jnp.sum(bl[2] * bl[3])) + lam_init
        lam_s = jnp.stack([lam, jnp.asarray(1.0 - lam_init, F32)])
        ob = _attn_b(lam_s, r3(qb), r3(kb), r3(vb), r3(zb), b_subln_g[l][None].astype(F32), tq_b, tk_b)

        qkv = _conv(cqkv, c_conv_w[l].astype(F32), seq, min(256, seq))
        bgt_c = bgt.reshape(N_BA, rows // C_CHUNK, C_CHUNK).transpose(1, 0, 2)
        ocf, ocb = _delta(r3(qkv), r3(bg), bgt_c)

        x2 = _merge(x2, oa.reshape(rows, A_WIDTH), ob.reshape(rows, B_WIDTH),
                    ocf.reshape(rows, C_WIDTH), ocb.reshape(rows, C_WIDTH), zc, gates,
                    c_norm_g[l][None].astype(F32), wa[l], wb[l], wc[l], wo[l],
                    final_g[None].astype(F32), tm, l == depth - 1)
    return x2.reshape(bsz, seq, D_MODEL)
```

```python
import functools
import math

import numpy as np
import jax
import jax.numpy as jnp
from jax import lax
from jax.experimental import pallas as pl
from jax.experimental.pallas import tpu as pltpu

F32 = jnp.float32
BF16 = jnp.bfloat16

D_MODEL = 1024
HEAD_DIM = 64
ROPE_THETA = 10000.0
EPS = 1e-6
A_Q_HEADS = 8
A_KV_HEADS = 2
A_WIDTH = A_Q_HEADS * HEAD_DIM
A_KV_WIDTH = A_KV_HEADS * HEAD_DIM
WINDOW = 128
B_HEADS = 4
B_V_DIM = 2 * HEAD_DIM
B_WIDTH = B_HEADS * B_V_DIM
C_HEADS = 4
C_HEAD_DIM = 128
C_WIDTH = C_HEADS * C_HEAD_DIM
C_CONV = 5
C_CHUNK = 64
N_BRANCH = 3
LOG2E = math.log2(math.e)
LANES = 128
SUBLANES = 8
MXU_N = 256
VMEM_LIMIT = 56 * 1024 * 1024

IN_SPLITS = (A_WIDTH, A_KV_WIDTH, A_KV_WIDTH, A_WIDTH,
             B_WIDTH, B_WIDTH, B_WIDTH, B_WIDTH,
             3 * C_WIDTH, C_WIDTH, 2 * C_HEADS, 2 * C_HEADS,
             N_BRANCH * D_MODEL)
_OFF = [0] + [int(o) for o in np.cumsum(IN_SPLITS)]

_PERM_A = np.concatenate([np.concatenate([np.arange(64 * b, 64 * b + 64),
                                          np.arange(64 * (4 + b), 64 * (4 + b) + 64)])
                          for b in range(4)])
_SEC = {}
_cols = []


def _add_section(name, idx):
    _SEC[name] = (sum(len(c) for c in _cols), len(idx))
    _cols.append(np.asarray(idx))


_add_section('aq', _OFF[0] + _PERM_A)
_add_section('ak', np.arange(_OFF[1], _OFF[2]))
_add_section('av', np.arange(_OFF[2], _OFF[3]))
_add_section('az', _OFF[3] + _PERM_A)
_add_section('bq', np.arange(_OFF[4], _OFF[5]))
_add_section('bk', np.arange(_OFF[5], _OFF[6]))
_add_section('bv', np.arange(_OFF[6], _OFF[7]))
_add_section('bz', np.arange(_OFF[7], _OFF[8]))
_add_section('cqkv', np.arange(_OFF[8], _OFF[9]))
_add_section('cz', np.arange(_OFF[9], _OFF[10]))
_add_section('gates', np.arange(_OFF[12], _OFF[13]))
_add_section('cba', np.arange(_OFF[10], _OFF[12]))
_COL_IDX = np.concatenate(_cols)
N_BA = 4 * C_HEADS
N_PROJ = len(_COL_IDX) - N_BA


def _cparams(sem):
    return pltpu.CompilerParams(dimension_semantics=sem, vmem_limit_bytes=VMEM_LIMIT)


def _rope(t, cos, sin_signed, first_half):
    swapped = jnp.where(first_half, pltpu.roll(t, LANES - 32, 1), pltpu.roll(t, 32, 1))
    return t * cos + swapped * sin_signed


def _softplus(t):
    return jnp.maximum(t, 0.0) + jnp.log(1.0 + jnp.exp(-jnp.abs(t)))


def _inproj_kernel(x_ref, g_ref, w_ref, wba_ref, cos_ref, sin_ref, dec_ref,
                   qa_ref, ka_ref, va_ref, za_ref, qb_ref, kb_ref, vb_ref, zb_ref,
                   cqkv_ref, zc_ref, gates_ref, bg_ref):
    x = x_ref[...]
    h = x * lax.rsqrt(jnp.mean(x * x, axis=-1, keepdims=True) + EPS) * g_ref[...]
    hb = h.astype(BF16)
    cos = cos_ref[...]
    sin = sin_ref[...]
    first_half = (lax.broadcasted_iota(jnp.int32, cos.shape, 1) & 32) == 0

    def rope(t):
        return _rope(t, cos, sin, first_half)

    def silu(t):
        return t * jax.nn.sigmoid(t)

    epilogues = (('aq', qa_ref, rope), ('ak', ka_ref, rope), ('av', va_ref, None), ('az', za_ref, silu),
                 ('bq', qb_ref, lambda t: rope(t) * LOG2E), ('bk', kb_ref, rope), ('bv', vb_ref, None),
                 ('bz', zb_ref, silu), ('cqkv', cqkv_ref, None), ('cz', zc_ref, silu),
                 ('gates', gates_ref, jax.nn.sigmoid))
    slabs = []
    for name, ref, fn in epilogues:
        assert _SEC[name][0] == len(slabs) * LANES
        slabs += [(ref, j, fn) for j in range(_SEC[name][1] // LANES)]
    for pair in range(len(slabs) // 2):
        acc = jnp.dot(hb, w_ref[:, pair * MXU_N:(pair + 1) * MXU_N], preferred_element_type=F32)
        for half in range(2):
            ref, j, fn = slabs[2 * pair + half]
            t = acc[:, half * LANES:(half + 1) * LANES]
            ref[:, j * LANES:(j + 1) * LANES] = (t if fn is None else fn(t)).astype(ref.dtype)

    ba = jnp.dot(hb, wba_ref[...], preferred_element_type=F32)
    is_beta = lax.broadcasted_iota(jnp.int32, ba.shape, 1) < 2 * C_HEADS
    dec = dec_ref[...]
    bg_ref[...] = jnp.where(is_beta, jax.nn.sigmoid(ba), dec[0:1, :] * _softplus(ba + dec[1:2, :]))


def _inproj(x2, g, w, wba, cos, sin, dec, seq, tm):
    rows = x2.shape[0]
    nseq = seq // tm
    row = lambda i: (i, 0)
    const = lambda i: (0, 0)
    single = pl.Buffered(1)
    out_widths = [('aq', BF16), ('ak', BF16), ('av', BF16), ('az', BF16), ('bq', BF16), ('bk', BF16),
                  ('bv', BF16), ('bz', BF16), ('cqkv', F32), ('cz', BF16), ('gates', BF16)]
    out_shape = [jax.ShapeDtypeStruct((rows, _SEC[n][1]), dt) for n, dt in out_widths]
    out_specs = [pl.BlockSpec((tm, _SEC[n][1]), row) for n, _ in out_widths]
    out_shape += [jax.ShapeDtypeStruct((rows, N_BA), F32)]
    out_specs += [pl.BlockSpec((tm, N_BA), row)]
    return pl.pallas_call(
        _inproj_kernel,
        grid=(rows // tm,),
        in_specs=[pl.BlockSpec((tm, D_MODEL), row),
                  pl.BlockSpec((1, D_MODEL), const),
                  pl.BlockSpec((D_MODEL, N_PROJ), const, pipeline_mode=single),
                  pl.BlockSpec((D_MODEL, N_BA), const),
                  pl.BlockSpec((tm, LANES), lambda i: (i % nseq, 0)),
                  pl.BlockSpec((tm, LANES), lambda i: (i % nseq, 0)),
                  pl.BlockSpec((2, N_BA), const)],
        out_specs=out_specs,
        out_shape=out_shape,
        compiler_params=_cparams(("parallel",)),
    )(x2, g, w, wba, cos, sin, dec)


def _attn_a_kernel(sink_ref, q_ref, kp_ref, k_ref, kn_ref, vp_ref, v_ref, vn_ref, z_ref, o_ref, *, tq, seq):
    i = pl.program_id(1)
    kcat = jnp.concatenate([kp_ref[0], k_ref[0], kn_ref[0]], axis=0)
    vcat = jnp.concatenate([vp_ref[0], v_ref[0], vn_ref[0]], axis=0)
    nk = tq + 2 * WINDOW
    lane = lax.broadcasted_iota(jnp.int32, (1, LANES), 1)
    low = lane < HEAD_DIM
    v_lo = jnp.where(low, vcat, jnp.zeros_like(vcat))
    v_hi = jnp.where(low, jnp.zeros_like(vcat), vcat)
    r = lax.broadcasted_iota(jnp.int32, (tq, nk), 0)
    c = lax.broadcasted_iota(jnp.int32, (tq, nk), 1)
    kabs = i * tq - WINDOW + c
    valid = (jnp.abs(c - WINDOW - r) <= WINDOW) & (kabs >= 0) & (kabs < seq)

    def probs(qm, sink):
        s = lax.dot_general(qm, kcat, (((1,), (1,)), ((), ())), preferred_element_type=F32)
        s = jnp.where(valid, s, -1e30)
        m = jnp.maximum(jnp.max(s, axis=-1, keepdims=True), sink)
        p = jnp.exp(s - m)
        l = jnp.sum(p, axis=-1, keepdims=True) + jnp.exp(sink - m)
        return (p / l).astype(BF16)

    for b in range(A_Q_HEADS // 2):
        q = q_ref[0, :, b * LANES:(b + 1) * LANES]
        p0 = probs(jnp.where(low, q, jnp.zeros_like(q)), sink_ref[b])
        p1 = probs(jnp.where(low, jnp.zeros_like(q), q), sink_ref[A_Q_HEADS // 2 + b])
        o = (jnp.dot(p0, v_lo, preferred_element_type=F32)
             + jnp.dot(p1, v_hi, preferred_element_type=F32))
        o_ref[0, :, b * LANES:(b + 1) * LANES] = (
            o * z_ref[0, :, b * LANES:(b + 1) * LANES].astype(F32)).astype(BF16)


def _attn_a(sink, q, k, v, z, tq):
    bsz, seq, _ = q.shape
    r = tq // WINDOW
    nwb = seq // WINDOW
    main = lambda b, i: (b, i, 0)
    prev = lambda b, i: (b, jnp.maximum(i * r - 1, 0), 0)
    nxt = lambda b, i: (b, jnp.minimum((i + 1) * r, nwb - 1), 0)
    kv_specs = [pl.BlockSpec((1, WINDOW, LANES), prev), pl.BlockSpec((1, tq, LANES), main),
                pl.BlockSpec((1, WINDOW, LANES), nxt)]
    return pl.pallas_call(
        functools.partial(_attn_a_kernel, tq=tq, seq=seq),
        grid=(bsz, seq // tq),
        in_specs=[pl.BlockSpec(memory_space=pltpu.SMEM), pl.BlockSpec((1, tq, A_WIDTH), main)]
                 + kv_specs + kv_specs + [pl.BlockSpec((1, tq, A_WIDTH), main)],
        out_specs=pl.BlockSpec((1, tq, A_WIDTH), main),
        out_shape=jax.ShapeDtypeStruct((bsz, seq, A_WIDTH), BF16),
        compiler_params=_cparams(("parallel", "parallel")),
    )(sink, q, k, k, k, v, v, v, z)


def _attn_b_kernel(lam_ref, q_ref, k_ref, v_ref, z_ref, g_ref, o_ref, acc_ref, m_ref, sa_ref, sb_ref,
                   *, tq, tk, seq):
    q = q_ref[0]
    low = lax.broadcasted_iota(jnp.int32, (1, LANES), 1) < HEAD_DIM
    zero = jnp.zeros_like(q)
    q2 = jnp.concatenate([jnp.where(low, q, zero), jnp.where(low, zero, q)], axis=0)
    acc_ref[...] = jnp.zeros_like(acc_ref)
    m_ref[...] = jnp.full_like(m_ref, -jnp.inf)
    ones = jnp.ones((tk, LANES), BF16)
    nk = seq // tk

    def scores(j):
        start = pl.multiple_of(j * tk, tk)
        return lax.dot_general(q2, k_ref[0, pl.ds(start, tk), :], (((1,), (1,)), ((), ())),
                               preferred_element_type=F32)

    def update(s_ref, j):
        start = pl.multiple_of(j * tk, tk)
        vc = jnp.concatenate([v_ref[0, pl.ds(start, tk), :], ones], axis=1)
        s = s_ref[...]
        m_old = m_ref[...]
        m_new = jnp.maximum(m_old, jnp.max(s, axis=-1, keepdims=True))
        p = jnp.exp2(s - m_new).astype(BF16)
        acc_ref[...] = acc_ref[...] * jnp.exp2(m_old - m_new) + jnp.dot(p, vc, preferred_element_type=F32)
        m_ref[...] = m_new

    sa_ref[...] = scores(0)

    def body(jj, carry):
        j = 2 * jj
        sb_ref[...] = scores(j + 1)
        update(sa_ref, j)
        sa_ref[...] = scores(jnp.minimum(j + 2, nk - 1))
        update(sb_ref, j + 1)
        return carry

    lax.fori_loop(0, nk // 2, body, 0)
    acc = acc_ref[...]
    o1 = acc[:tq, :LANES] / acc[:tq, LANES:]
    o2 = acc[tq:, :LANES] / acc[tq:, LANES:]
    o = o1 - lam_ref[0] * o2
    o = o * lax.rsqrt(jnp.mean(o * o, axis=-1, keepdims=True) + EPS) * g_ref[...] * lam_ref[1]
    o_ref[0] = (o * z_ref[0].astype(F32)).astype(BF16)


def _attn_b(lam, q, k, v, z, g, tq, tk):
    bsz, seq, _ = q.shape
    assert seq % (2 * tk) == 0, "key chunks are consumed in pairs"
    qmap = lambda b, h, i: (b, i, h)
    kmap = lambda b, h, i: (b, 0, h)
    return pl.pallas_call(
        functools.partial(_attn_b_kernel, tq=tq, tk=tk, seq=seq),
        grid=(bsz, B_HEADS, seq // tq),
        in_specs=[pl.BlockSpec(memory_space=pltpu.SMEM),
                  pl.BlockSpec((1, tq, LANES), qmap),
                  pl.BlockSpec((1, seq, LANES), kmap),
                  pl.BlockSpec((1, seq, LANES), kmap),
                  pl.BlockSpec((1, tq, LANES), qmap),
                  pl.BlockSpec((1, LANES), lambda b, h, i: (0, 0))],
        out_specs=pl.BlockSpec((1, tq, LANES), qmap),
        out_shape=jax.ShapeDtypeStruct((bsz, seq, B_WIDTH), BF16),
        scratch_shapes=[pltpu.VMEM((2 * tq, 2 * LANES), F32), pltpu.VMEM((2 * tq, 1), F32),
                        pltpu.VMEM((2 * tq, tk), F32), pltpu.VMEM((2 * tq, tk), F32)],
        compiler_params=_cparams(("parallel", "parallel", "parallel")),
    )(lam, q, k, v, z, g)


def _conv_kernel(xp_ref, x_ref, xn_ref, w_ref, o_ref, buf_ref, *, tc, seq):
    i = pl.program_id(0)
    pos = (i * tc) % seq
    halo = (C_CONV - 1) // 2
    buf_ref[0:SUBLANES, :] = jnp.where(pos == 0, 0.0, xp_ref[...])
    buf_ref[SUBLANES:SUBLANES + tc, :] = x_ref[...]
    buf_ref[SUBLANES + tc:, :] = jnp.where(pos + tc == seq, 0.0, xn_ref[...])
    w = w_ref[...]
    y = jnp.zeros((tc, 3 * C_WIDTH), F32)
    for t in range(C_CONV):
        y = y + buf_ref[SUBLANES - halo + t:SUBLANES - halo + t + tc, :] * w[t:t + 1, :]
    y = y * jax.nn.sigmoid(y)
    for j in range(3 * C_HEADS):
        blk = y[:, j * LANES:(j + 1) * LANES]
        if j < 2 * C_HEADS:
            blk = blk * lax.rsqrt(jnp.sum(blk * blk, axis=-1, keepdims=True) + EPS)
        if j < C_HEADS:
            blk = blk * (C_HEAD_DIM ** -0.5)
        o_ref[:, j * LANES:(j + 1) * LANES] = blk


def _conv(x2, w, seq, tc):
    rows, width = x2.shape
    r = tc // SUBLANES
    nb8 = rows // SUBLANES
    return pl.pallas_call(
        functools.partial(_conv_kernel, tc=tc, seq=seq),
        grid=(rows // tc,),
        in_specs=[pl.BlockSpec((SUBLANES, width), lambda i: (jnp.maximum(i * r - 1, 0), 0)),
                  pl.BlockSpec((tc, width), lambda i: (i, 0)),
                  pl.BlockSpec((SUBLANES, width), lambda i: (jnp.minimum((i + 1) * r, nb8 - 1), 0)),
                  pl.BlockSpec((C_CONV, width), lambda i: (0, 0))],
        out_specs=pl.BlockSpec((tc, width), lambda i: (i, 0)),
        out_shape=jax.ShapeDtypeStruct((rows, width), F32),
        scratch_shapes=[pltpu.VMEM((tc + 2 * SUBLANES, width), F32)],
        compiler_params=_cparams(("parallel",)),
    )(x2, x2, x2, w)


HC = C_HEADS * C_CHUNK
CHUNK_SHIFT = 6


def _mm(a, b):
    return jnp.dot(a, b, preferred_element_type=F32)


def _split3(x):
    hi = x.astype(BF16)
    r = x - hi.astype(F32)
    mid = r.astype(BF16)
    return hi, mid, (r - mid.astype(F32)).astype(BF16)


def _block_diag(x, mask):
    return jnp.where(mask, jnp.concatenate([x] * C_HEADS, axis=0), jnp.zeros((), x.dtype))


def _delta_prep_kernel(qkv_ref, bg_ref, w_ref, u_ref, qd_ref, ktt_ref, att_ref, eg_ref, *, rb):
    c = C_CHUNK
    bg = bg_ref[...]
    g3 = _split3(bg)
    rr = lax.broadcasted_iota(jnp.int32, (rb, rb), 0)
    cc = lax.broadcasted_iota(jnp.int32, (rb, rb), 1)
    same = (rr >> CHUNK_SHIFT) == (cc >> CHUNK_SHIFT)
    tri = (jnp.where(same & (rr >= cc), 1.0, 0.0).astype(BF16),
           jnp.where(same & (rr <= cc), 1.0, 0.0).astype(BF16))
    gcum = [_mm(t, g3[0]) + _mm(t, g3[1]) + _mm(t, g3[2]) for t in tri]

    row = lax.broadcasted_iota(jnp.int32, (c, HC), 0)
    lm = lax.broadcasted_iota(jnp.int32, (c, HC), 1) & (c - 1)
    eye_cat = row == lm
    lane_lo = lax.broadcasted_iota(jnp.int32, (c, LANES), 1) < c
    bdmask = ((lax.broadcasted_iota(jnp.int32, (HC, HC), 0) >> CHUNK_SHIFT)
              == (lax.broadcasted_iota(jnp.int32, (HC, HC), 1) >> CHUNK_SHIFT))
    zeros_blk = jnp.zeros((c, LANES), BF16)
    heads = range(C_HEADS)

    def col(a, j):
        return jnp.broadcast_to(a[:, j:j + 1], (c, LANES))

    def spread(cols):
        return jnp.concatenate([jnp.where(lane_lo, cols[0], cols[1]),
                                jnp.where(lane_lo, cols[2], cols[3])], axis=1)

    amats = []
    rhss = []
    for ci in range(rb // c):
        rows = slice(ci * c, (ci + 1) * c)
        q = [qkv_ref[rows, h * LANES:(h + 1) * LANES] for h in heads]
        k = [qkv_ref[rows, C_WIDTH + h * LANES:C_WIDTH + (h + 1) * LANES] for h in heads]
        v = [qkv_ref[rows, 2 * C_WIDTH + h * LANES:2 * C_WIDTH + (h + 1) * LANES] for h in heads]
        kb = [t.astype(BF16) for t in k]
        lhs = jnp.concatenate([jnp.concatenate(kb, axis=1),
                               jnp.concatenate([t.astype(BF16) for t in q], axis=1)], axis=0)
        kmask = jnp.concatenate(
            [jnp.concatenate([kb[h] if j == h else zeros_blk for j in heads], axis=1) for h in heads],
            axis=0)
        gram = lax.dot_general(lhs, kmask, (((1,), (1,)), ((), ())), preferred_element_type=F32)
        kk = gram[:c]
        qk = gram[c:]
        beta_all = bg[rows]
        for d in range(2):
            gc = gcum[d][rows]
            off_b = d * C_HEADS
            off_g = 2 * C_HEADS + d * C_HEADS
            bcol = [col(beta_all, off_b + h) for h in heads]
            gcol = [col(gc, off_g + h) for h in heads]
            g_last = gc[c - 1:c, :] if d == 0 else gc[0:1, :]
            glcol = [jnp.broadcast_to(g_last[:, off_g + h:off_g + h + 1], (c, LANES)) for h in heads]
            gc_sp = spread(gcol)
            gc_row = jnp.sum(jnp.where(eye_cat, gc_sp, 0.0), axis=0, keepdims=True)
            incl = (row >= lm) if d == 0 else (row <= lm)
            strict = (row > lm) if d == 0 else (row < lm)
            decay = jnp.where(incl, jnp.exp(jnp.where(incl, gc_sp - gc_row, 0.0)), 0.0)
            amats.append(jnp.where(strict, spread(bcol) * kk * decay, 0.0))
            egc = [jnp.exp(gcol[h]) for h in heads]
            rhss.append(jnp.concatenate(
                [jnp.concatenate([v[h] * bcol[h], k[h] * (bcol[h] * egc[h])], axis=1) for h in heads],
                axis=0))
            qd_ref[ci, d] = jnp.concatenate([q[h] * egc[h] for h in heads], axis=0).astype(BF16)
            kt = jnp.concatenate([k[h] * jnp.exp(glcol[h] - gcol[h]) for h in heads], axis=0)
            ktt_ref[ci, d] = kt.T.astype(BF16)
            att_ref[ci, d] = jnp.where(incl, qk * decay, 0.0).astype(BF16)
            eg_ref[ci, d] = jnp.exp(jnp.concatenate([glcol[h][0:1, :] for h in heads], axis=0))

    units = range(len(amats))
    tms = [-a for a in amats]
    abs_ = [a.astype(BF16) for a in amats]
    pws = [_mm(abs_[i], _block_diag(abs_[i], bdmask)) for i in units]
    for _ in range(CHUNK_SHIFT - 2):
        pbs = [p.astype(BF16) for p in pws]
        both = [_mm(jnp.concatenate([tms[i].astype(BF16), pbs[i]], axis=0), _block_diag(pbs[i], bdmask))
                for i in units]
        tms = [tms[i] + pws[i] + both[i][:c] for i in units]
        pws = [both[i][c:] for i in units]
    last = [_mm(tms[i].astype(BF16), _block_diag(pws[i].astype(BF16), bdmask)) for i in units]
    tms = [tms[i] + pws[i] + last[i] for i in units]
    uws = [_mm(_block_diag(tms[i].astype(BF16), bdmask), rhss[i].astype(BF16)) for i in units]
    for i in units:
        uw = rhss[i] + uws[i]
        u_ref[i // 2, i % 2] = uw[:, :LANES]
        w_ref[i // 2, i % 2] = uw[:, LANES:].astype(BF16)


def _delta_prep(qkv2, bg2, rb):
    rows = qkv2.shape[0]
    nct = rows // C_CHUNK
    nc = rb // C_CHUNK
    blk = lambda i: (i, 0, 0, 0)
    shapes = [((HC, LANES), BF16), ((HC, LANES), F32), ((HC, LANES), BF16),
              ((C_HEAD_DIM, HC), BF16), ((C_CHUNK, HC), BF16), ((C_HEADS, LANES), F32)]
    return pl.pallas_call(
        functools.partial(_delta_prep_kernel, rb=rb),
        grid=(rows // rb,),
        in_specs=[pl.BlockSpec((rb, 3 * C_WIDTH), lambda i: (i, 0)),
                  pl.BlockSpec((rb, N_BA), lambda i: (i, 0))],
        out_specs=[pl.BlockSpec((nc, 2) + s, blk) for s, _ in shapes],
        out_shape=[jax.ShapeDtypeStruct((nct, 2) + s, dt) for s, dt in shapes],
        compiler_params=_cparams(("parallel",)),
    )(qkv2, bg2)


def _delta_scan_kernel(*refs, cb, bsz):
    ins = (refs[0:6], refs[6:12])
    outs = refs[12:14]
    state_ref = refs[14]
    c = C_CHUNK

    @pl.when(pl.program_id(0) == 0)
    def _():
        state_ref[...] = jnp.zeros_like(state_ref)

    bdmask = ((lax.broadcasted_iota(jnp.int32, (HC, HC), 0) >> CHUNK_SHIFT)
              == (lax.broadcasted_iota(jnp.int32, (HC, HC), 1) >> CHUNK_SHIFT))
    lane_head = lax.broadcasted_iota(jnp.int32, (1, HC), 1) >> CHUNK_SHIFT
    heads = range(C_HEADS)
    for step in range(cb):
        for b in range(bsz):
            for d in range(2):
                w_ref, u_ref, qd_ref, ktt_ref, att_ref, eg_ref = ins[d]
                ci = step if d == 0 else cb - 1 - step
                w = w_ref[b, ci]
                qd = qd_ref[b, ci]
                ktt = ktt_ref[b, ci]
                eg = eg_ref[b, ci]
                base = (b * 2 + d) * C_HEADS
                state = [state_ref[base + h] for h in heads]
                prod = [_mm(jnp.concatenate([w[h * c:(h + 1) * c], qd[h * c:(h + 1) * c]], axis=0),
                            state[h].astype(BF16)) for h in heads]
                ws = jnp.concatenate([p[:c] for p in prod], axis=0)
                qs = jnp.concatenate([p[c:] for p in prod], axis=0)
                vnb = (u_ref[b, ci] - ws).astype(BF16)
                o = qs + _mm(_block_diag(att_ref[b, ci], bdmask), vnb)
                for h in heads:
                    kth = jnp.where(lane_head == h, ktt, jnp.zeros((), ktt.dtype))
                    state_ref[base + h] = state[h] * eg[h:h + 1, :] + _mm(kth, vnb)
                    outs[d][b, ci * c:(ci + 1) * c, h * LANES:(h + 1) * LANES] = o[h * c:(h + 1) * c]


def _delta_scan(prep, bsz, seq, cb):
    nch = seq // C_CHUNK
    nblk = nch // cb
    arrs = [a.reshape((bsz, nch) + a.shape[1:]) for a in prep]

    def specs(d):
        idx = (lambda n: (0, n, d, 0, 0)) if d == 0 else (lambda n: (0, nblk - 1 - n, d, 0, 0))
        return [pl.BlockSpec((bsz, cb, None) + a.shape[3:], idx) for a in arrs]

    out = jax.ShapeDtypeStruct((bsz, seq, C_WIDTH), F32)
    return pl.pallas_call(
        functools.partial(_delta_scan_kernel, cb=cb, bsz=bsz),
        grid=(nblk,),
        in_specs=specs(0) + specs(1),
        out_specs=[pl.BlockSpec((bsz, cb * C_CHUNK, C_WIDTH), lambda n: (0, n, 0)),
                   pl.BlockSpec((bsz, cb * C_CHUNK, C_WIDTH), lambda n: (0, nblk - 1 - n, 0))],
        out_shape=[out, out],
        scratch_shapes=[pltpu.VMEM((bsz * 2 * C_HEADS, C_HEAD_DIM, C_HEAD_DIM), F32)],
        compiler_params=_cparams(("arbitrary",)),
    )(*arrs, *arrs)


def _merge_kernel(x_ref, oa_ref, ob_ref, ocf_ref, ocb_ref, zc_ref, gates_ref, cg_ref, wa_ref, wb_ref, wc_ref,
                  wo_ref, fg_ref, o_ref, *, final):
    oc = ocf_ref[...] + ocb_ref[...]
    cg = cg_ref[...]
    parts = []
    for h in range(C_HEADS):
        blk = oc[:, h * LANES:(h + 1) * LANES]
        blk = blk * lax.rsqrt(jnp.mean(blk * blk, axis=-1, keepdims=True) + EPS) * cg
        parts.append((blk * zc_ref[:, h * LANES:(h + 1) * LANES].astype(F32)).astype(BF16))
    ocn = jnp.concatenate(parts, axis=1)
    ya = jnp.dot(oa_ref[...], wa_ref[...], preferred_element_type=F32)
    yb = jnp.dot(ob_ref[...], wb_ref[...], preferred_element_type=F32)
    yc = jnp.dot(ocn, wc_ref[...], preferred_element_type=F32)
    d = D_MODEL
    merged = (gates_ref[:, 0:d].astype(F32) * ya + gates_ref[:, d:2 * d].astype(F32) * yb
              + gates_ref[:, 2 * d:3 * d].astype(F32) * yc)
    x = x_ref[...] + jnp.dot(merged.astype(BF16), wo_ref[...], preferred_element_type=F32)
    if final:
        x = x * lax.rsqrt(jnp.mean(x * x, axis=-1, keepdims=True) + EPS) * fg_ref[...]
    o_ref[...] = x


def _merge(x2, oa, ob, ocf, ocb, zc, gates, cg, wa, wb, wc, wo, fg, tm, final):
    rows = x2.shape[0]
    row = lambda i: (i, 0)
    const = lambda i: (0, 0)
    return pl.pallas_call(
        functools.partial(_merge_kernel, final=final),
        grid=(rows // tm,),
        in_specs=[pl.BlockSpec((tm, D_MODEL), row),
                  pl.BlockSpec((tm, A_WIDTH), row), pl.BlockSpec((tm, B_WIDTH), row),
                  pl.BlockSpec((tm, C_WIDTH), row), pl.BlockSpec((tm, C_WIDTH), row),
                  pl.BlockSpec((tm, C_WIDTH), row), pl.BlockSpec((tm, N_BRANCH * D_MODEL), row),
                  pl.BlockSpec((1, LANES), const),
                  pl.BlockSpec((A_WIDTH, D_MODEL), const), pl.BlockSpec((B_WIDTH, D_MODEL), const),
                  pl.BlockSpec((C_WIDTH, D_MODEL), const), pl.BlockSpec((D_MODEL, D_MODEL), const),
                  pl.BlockSpec((1, D_MODEL), const)],
        out_specs=pl.BlockSpec((tm, D_MODEL), row),
        out_shape=jax.ShapeDtypeStruct((rows, D_MODEL), F32),
        compiler_params=_cparams(("parallel",)),
    )(x2, oa, ob, ocf, ocb, zc, gates, cg, wa, wb, wc, wo, fg)


def _rope_tables(seq):
    inv = 1.0 / (ROPE_THETA ** (jnp.arange(0, HEAD_DIM, 2, dtype=F32) / HEAD_DIM))
    ang = jnp.arange(seq, dtype=F32)[:, None] * inv[None, :]
    cos = jnp.tile(jnp.cos(ang), (1, LANES // (HEAD_DIM // 2)))
    sin = jnp.tile(jnp.concatenate([-jnp.sin(ang), jnp.sin(ang)], axis=1), (1, LANES // HEAD_DIM))
    return cos, sin


def kernel(x, norm_g, w_in, a_sink, b_lambda, b_subln_g, c_conv_w, c_a_log, c_dt_bias, c_norm_g,
           w_bo_a, w_bo_b, w_bo_c, w_out, final_g):
    bsz, seq, _ = x.shape
    depth = w_in.shape[0]
    rows = bsz * seq
    tm = min(256, seq)
    tq_a = min(256, seq)
    tq_b = min(512, seq)
    tk_b = min(1024, seq // 2)
    cos, sin = _rope_tables(seq)

    scale = np.ones((len(_COL_IDX),), np.float32)
    for name in ('aq', 'bq'):
        lo, n = _SEC[name]
        scale[lo:lo + n] = HEAD_DIM ** -0.5
    w_all = jnp.take(w_in, jnp.asarray(_COL_IDX), axis=2) * jnp.asarray(scale)
    w_main = w_all[:, :, :N_PROJ].astype(BF16)
    w_ba = w_all[:, :, N_PROJ:].astype(BF16)
    dec = jnp.stack([jnp.concatenate([jnp.zeros((depth, 2 * C_HEADS), F32),
                                      -jnp.exp(c_a_log.astype(F32)).reshape(depth, 2 * C_HEADS)], axis=1),
                     jnp.concatenate([jnp.zeros((depth, 2 * C_HEADS), F32),
                                      c_dt_bias.astype(F32).reshape(depth, 2 * C_HEADS)], axis=1)], axis=1)
    wa =jnp.take(w_bo_a, jnp.asarray(_PERM_A), axis=1).astype(BF16)
    wb = w_bo_b.astype(BF16)
    wc = w_bo_c.astype(BF16)
    wo = w_out.astype(BF16)

    x2 = x.reshape(rows, D_MODEL)
    for l in range(depth):
        (qa, ka, va, za, qb, kb, vb, zb, cqkv, zc, gates, bg) = _inproj(
            x2, norm_g[l][None], w_main[l], w_ba[l], cos, sin, dec[l], seq, tm)

        r3 = lambda t: t.reshape(bsz, seq, t.shape[-1])
        oa = _attn_a(a_sink[l].astype(F32), r3(qa), r3(ka), r3(va), r3(za), tq_a)

        lam_init = 0.8 - 0.6 * math.exp(-0.3 * l)
        bl = b_lambda[l].astype(F32)
        lam = jnp.exp(jnp.sum(bl[0] * bl[1])) - jnp.exp(jnp.sum(bl[2] * bl[3])) + lam_init
        lam_s = jnp.stack([lam, jnp.asarray(1.0 - lam_init, F32)])
        ob = _attn_b(lam_s, r3(qb), r3(kb), r3(vb), r3(zb), b_subln_g[l][None].astype(F32), tq_b, tk_b)

        qkv = _conv(cqkv, c_conv_w[l].astype(F32), seq, min(256, seq))
        prep = _delta_prep(qkv, bg, min(256, seq))
        ocf, ocb = _delta_scan(prep, bsz, seq, 2)

        x2 = _merge(x2, oa.reshape(rows, A_WIDTH), ob.reshape(rows, B_WIDTH),
                    ocf.reshape(rows, C_WIDTH), ocb.reshape(rows, C_WIDTH), zc, gates,
                    c_norm_g[l][None].astype(F32), wa[l], wb[l], wc[l], wo[l],
                    final_g[None].astype(F32), tm, l == depth - 1)
    return x2.reshape(bsz, seq, D_MODEL)
```

```python
import functools
import math

import numpy as np
import jax
import jax.numpy as jnp
from jax import lax
from jax.experimental import pallas as pl
from jax.experimental.pallas import tpu as pltpu

F32 = jnp.float32
BF16 = jnp.bfloat16

D_MODEL = 1024
HEAD_DIM = 64
ROPE_THETA = 10000.0
EPS = 1e-6
A_Q_HEADS = 8
A_KV_HEADS = 2
A_WIDTH = A_Q_HEADS * HEAD_DIM
A_KV_WIDTH = A_KV_HEADS * HEAD_DIM
WINDOW = 128
B_HEADS = 4
B_V_DIM = 2 * HEAD_DIM
B_WIDTH = B_HEADS * B_V_DIM
C_HEADS = 4
C_HEAD_DIM = 128
C_WIDTH = C_HEADS * C_HEAD_DIM
C_CONV = 5
C_CHUNK = 64
N_BRANCH = 3
LOG2E = math.log2(math.e)
LANES = 128
SUBLANES = 8
MXU_N = 256
VMEM_LIMIT = 56 * 1024 * 1024

IN_SPLITS = (A_WIDTH, A_KV_WIDTH, A_KV_WIDTH, A_WIDTH,
             B_WIDTH, B_WIDTH, B_WIDTH, B_WIDTH,
             3 * C_WIDTH, C_WIDTH, 2 * C_HEADS, 2 * C_HEADS,
             N_BRANCH * D_MODEL)
_OFF = [0] + [int(o) for o in np.cumsum(IN_SPLITS)]

_PERM_A = np.concatenate([np.concatenate([np.arange(64 * b, 64 * b + 64),
                                          np.arange(64 * (4 + b), 64 * (4 + b) + 64)])
                          for b in range(4)])
_SEC = {}
_cols = []


def _add_section(name, idx):
    _SEC[name] = (sum(len(c) for c in _cols), len(idx))
    _cols.append(np.asarray(idx))


_add_section('aq', _OFF[0] + _PERM_A)
_add_section('ak', np.arange(_OFF[1], _OFF[2]))
_add_section('av', np.arange(_OFF[2], _OFF[3]))
_add_section('az', _OFF[3] + _PERM_A)
_add_section('bq', np.arange(_OFF[4], _OFF[5]))
_add_section('bk', np.arange(_OFF[5], _OFF[6]))
_add_section('bv', np.arange(_OFF[6], _OFF[7]))
_add_section('bz', np.arange(_OFF[7], _OFF[8]))
_add_section('cqkv', np.arange(_OFF[8], _OFF[9]))
_add_section('cz', np.arange(_OFF[9], _OFF[10]))
_add_section('gates', np.arange(_OFF[12], _OFF[13]))
_add_section('cba', np.arange(_OFF[10], _OFF[12]))
_COL_IDX = np.concatenate(_cols)
N_BA = 4 * C_HEADS
N_PROJ = len(_COL_IDX) - N_BA


def _cparams(sem):
    return pltpu.CompilerParams(dimension_semantics=sem, vmem_limit_bytes=VMEM_LIMIT)


def _rope(t, cos, sin_signed, first_half):
    swapped = jnp.where(first_half, pltpu.roll(t, LANES - 32, 1), pltpu.roll(t, 32, 1))
    return t * cos + swapped * sin_signed


def _softplus(t):
    return jnp.maximum(t, 0.0) + jnp.log(1.0 + jnp.exp(-jnp.abs(t)))


def _inproj_kernel(x_ref, g_ref, w_ref, wba_ref, cos_ref, sin_ref, dec_ref,
                   qa_ref, ka_ref, va_ref, za_ref, qb_ref, kb_ref, vb_ref, zb_ref,
                   cqkv_ref, zc_ref, gates_ref, bg_ref):
    x = x_ref[...]
    h = x * lax.rsqrt(jnp.mean(x * x, axis=-1, keepdims=True) + EPS) * g_ref[...]
    hb = h.astype(BF16)
    cos = cos_ref[...]
    sin = sin_ref[...]
    first_half = (lax.broadcasted_iota(jnp.int32, cos.shape, 1) & 32) == 0

    def rope(t):
        return _rope(t, cos, sin, first_half)

    def silu(t):
        return t * jax.nn.sigmoid(t)

    def rope_q(t):
        return rope(t) * LOG2E

    epilogues = (('aq', qa_ref, rope_q), ('ak', ka_ref, rope), ('av', va_ref, None), ('az', za_ref, silu),
                 ('bq', qb_ref, rope_q), ('bk', kb_ref, rope), ('bv', vb_ref, None),
                 ('bz', zb_ref, silu), ('cqkv', cqkv_ref, None), ('cz', zc_ref, silu),
                 ('gates', gates_ref, jax.nn.sigmoid))
    slabs = []
    for name, ref, fn in epilogues:
        assert _SEC[name][0] == len(slabs) * LANES
        slabs += [(ref, j, fn) for j in range(_SEC[name][1] // LANES)]
    for pair in range(len(slabs) // 2):
        acc = jnp.dot(hb, w_ref[:, pair * MXU_N:(pair + 1) * MXU_N], preferred_element_type=F32)
        for half in range(2):
            ref, j, fn = slabs[2 * pair + half]
            t = acc[:, half * LANES:(half + 1) * LANES]
            ref[:, j * LANES:(j + 1) * LANES] = (t if fn is None else fn(t)).astype(ref.dtype)

    ba = jnp.dot(hb, wba_ref[...], preferred_element_type=F32)
    is_beta = lax.broadcasted_iota(jnp.int32, ba.shape, 1) < 2 * C_HEADS
    dec = dec_ref[...]
    bg_ref[...] = jnp.where(is_beta, jax.nn.sigmoid(ba), dec[0:1, :] * _softplus(ba + dec[1:2, :]))


def _inproj(x2, g, w, wba, cos, sin, dec, seq, tm):
    rows = x2.shape[0]
    nseq = seq // tm
    row = lambda i: (i, 0)
    const = lambda i: (0, 0)
    single = pl.Buffered(1)
    out_widths = [('aq', BF16), ('ak', BF16), ('av', BF16), ('az', BF16), ('bq', BF16), ('bk', BF16),
                  ('bv', BF16), ('bz', BF16), ('cqkv', F32), ('cz', BF16), ('gates', BF16)]
    out_shape = [jax.ShapeDtypeStruct((rows, _SEC[n][1]), dt) for n, dt in out_widths]
    out_specs = [pl.BlockSpec((tm, _SEC[n][1]), row) for n, _ in out_widths]
    out_shape += [jax.ShapeDtypeStruct((rows, N_BA), F32)]
    out_specs += [pl.BlockSpec((tm, N_BA), row)]
    return pl.pallas_call(
        _inproj_kernel,
        grid=(rows // tm,),
        in_specs=[pl.BlockSpec((tm, D_MODEL), row),
                  pl.BlockSpec((1, D_MODEL), const),
                  pl.BlockSpec((D_MODEL, N_PROJ), const, pipeline_mode=single),
                  pl.BlockSpec((D_MODEL, N_BA), const),
                  pl.BlockSpec((tm, LANES), lambda i: (i % nseq, 0)),
                  pl.BlockSpec((tm, LANES), lambda i: (i % nseq, 0)),
                  pl.BlockSpec((2, N_BA), const)],
        out_specs=out_specs,
        out_shape=out_shape,
        compiler_params=_cparams(("parallel",)),
    )(x2, g, w, wba, cos, sin, dec)


def _attn_a_kernel(sink_ref, q_ref, kp_ref, k_ref, kn_ref, vp_ref, v_ref, vn_ref, z_ref, o_ref, *, tq, seq):
    i = pl.program_id(1)
    kcat = jnp.concatenate([kp_ref[0], k_ref[0], kn_ref[0]], axis=0)
    vcat = jnp.concatenate([vp_ref[0], v_ref[0], vn_ref[0]], axis=0)
    nk = tq + 2 * WINDOW
    lane = lax.broadcasted_iota(jnp.int32, (1, LANES), 1)
    low = lane < HEAD_DIM
    one = jnp.ones_like(vcat)
    v_lo = jnp.where(low, vcat, one)
    v_hi = jnp.where(low, one, vcat)
    r = lax.broadcasted_iota(jnp.int32, (tq, nk), 0)
    c = lax.broadcasted_iota(jnp.int32, (tq, nk), 1)
    kabs = i * tq - WINDOW + c
    valid = (jnp.abs(c - WINDOW - r) <= WINDOW) & (kabs >= 0) & (kabs < seq)

    def probs(qm, sink):
        s = lax.dot_general(qm, kcat, (((1,), (1,)), ((), ())), preferred_element_type=F32)
        s = jnp.where(valid, s, -1e30)
        m = jnp.maximum(jnp.max(s, axis=-1, keepdims=True), sink)
        return jnp.exp2(s - m).astype(BF16), jnp.exp2(sink - m)

    half = A_Q_HEADS // 2
    for b in range(half):
        q = q_ref[0, :, b * LANES:(b + 1) * LANES]
        p0, e0 = probs(jnp.where(low, q, jnp.zeros_like(q)), sink_ref[b] * LOG2E)
        p1, e1 = probs(jnp.where(low, jnp.zeros_like(q), q), sink_ref[half + b] * LOG2E)
        r0 = jnp.dot(p0, v_lo, preferred_element_type=F32)
        r1 = jnp.dot(p1, v_hi, preferred_element_type=F32)
        num = jnp.where(low, r0, r1)
        den = (jnp.where(low, pltpu.roll(r0, HEAD_DIM, 1), pltpu.roll(r1, HEAD_DIM, 1))
               + jnp.where(low, e0, e1))
        o_ref[0, :, b * LANES:(b + 1) * LANES] = (
            num / den * z_ref[0, :, b * LANES:(b + 1) * LANES].astype(F32)).astype(BF16)


def _attn_a(sink, q, k, v, z, tq):
    bsz, seq, _ = q.shape
    r = tq // WINDOW
    nwb = seq // WINDOW
    main = lambda b, i: (b, i, 0)
    prev = lambda b, i: (b, jnp.maximum(i * r - 1, 0), 0)
    nxt = lambda b, i: (b, jnp.minimum((i + 1) * r, nwb - 1), 0)
    kv_specs = [pl.BlockSpec((1, WINDOW, LANES), prev), pl.BlockSpec((1, tq, LANES), main),
                pl.BlockSpec((1, WINDOW, LANES), nxt)]
    return pl.pallas_call(
        functools.partial(_attn_a_kernel, tq=tq, seq=seq),
        grid=(bsz, seq // tq),
        in_specs=[pl.BlockSpec(memory_space=pltpu.SMEM), pl.BlockSpec((1, tq, A_WIDTH), main)]
                 + kv_specs + kv_specs + [pl.BlockSpec((1, tq, A_WIDTH), main)],
        out_specs=pl.BlockSpec((1, tq, A_WIDTH), main),
        out_shape=jax.ShapeDtypeStruct((bsz, seq, A_WIDTH), BF16),
        compiler_params=_cparams(("parallel", "parallel")),
    )(sink, q, k, k, k, v, v, v, z)


def _attn_b_kernel(lam_ref, q_ref, k_ref, v_ref, z_ref, g_ref, o_ref, acc_ref, m_ref, sa_ref, sb_ref,
                   *, tq, tk, seq):
    q = q_ref[0]
    low = lax.broadcasted_iota(jnp.int32, (1, LANES), 1) < HEAD_DIM
    zero = jnp.zeros_like(q)
    q2 = jnp.concatenate([jnp.where(low, q, zero), jnp.where(low, zero, q)], axis=0)
    acc_ref[...] = jnp.zeros_like(acc_ref)
    m_ref[...] = jnp.full_like(m_ref, -jnp.inf)
    ones = jnp.ones((tk, LANES), BF16)
    nk = seq // tk

    def scores(j):
        return lax.dot_general(q2, k_ref[0, j * tk:(j + 1) * tk, :], (((1,), (1,)), ((), ())),
                               preferred_element_type=F32)

    def update(s_ref, j):
        vc = jnp.concatenate([v_ref[0, j * tk:(j + 1) * tk, :], ones], axis=1)
        s = s_ref[...]
        m_old = m_ref[...]
        m_new = jnp.maximum(m_old, jnp.max(s, axis=-1, keepdims=True))
        p = jnp.exp2(s - m_new).astype(BF16)
        acc_ref[...] = acc_ref[...] * jnp.exp2(m_old - m_new) + jnp.dot(p, vc, preferred_element_type=F32)
        m_ref[...] = m_new

    bufs = (sa_ref, sb_ref)
    sa_ref[...] = scores(0)
    for j in range(nk):
        if j + 1 < nk:
            bufs[(j + 1) % 2][...] = scores(j + 1)
        update(bufs[j % 2], j)
    acc = acc_ref[...]
    o1 = acc[:tq, :LANES] / acc[:tq, LANES:]
    o2 = acc[tq:, :LANES] / acc[tq:, LANES:]
    o = o1 - lam_ref[0] * o2
    o = o * lax.rsqrt(jnp.mean(o * o, axis=-1, keepdims=True) + EPS) * g_ref[...] * lam_ref[1]
    o_ref[0] = (o * z_ref[0].astype(F32)).astype(BF16)


def _attn_b(lam, q, k, v, z, g, tq, tk):
    bsz, seq, _ = q.shape
    assert seq % tk == 0 and seq % tq == 0
    qmap = lambda b, h, i: (b, i, h)
    kmap = lambda b, h, i: (b, 0, h)
    return pl.pallas_call(
        functools.partial(_attn_b_kernel, tq=tq, tk=tk, seq=seq),
        grid=(bsz, B_HEADS, seq // tq),
        in_specs=[pl.BlockSpec(memory_space=pltpu.SMEM),
                  pl.BlockSpec((1, tq, LANES), qmap),
                  pl.BlockSpec((1, seq, LANES), kmap),
                  pl.BlockSpec((1, seq, LANES), kmap),
                  pl.BlockSpec((1, tq, LANES), qmap),
                  pl.BlockSpec((1, LANES), lambda b, h, i: (0, 0))],
        out_specs=pl.BlockSpec((1, tq, LANES), qmap),
        out_shape=jax.ShapeDtypeStruct((bsz, seq, B_WIDTH), BF16),
        scratch_shapes=[pltpu.VMEM((2 * tq, 2 * LANES), F32), pltpu.VMEM((2 * tq, 1), F32),
                        pltpu.VMEM((2 * tq, tk), F32), pltpu.VMEM((2 * tq, tk), F32)],
        compiler_params=_cparams(("parallel", "parallel", "parallel")),
    )(lam, q, k, v, z, g)


def _conv_kernel(xp_ref, x_ref, xn_ref, w_ref, o_ref, buf_ref, *, tc, seq):
    i = pl.program_id(0)
    pos = (i * tc) % seq
    halo = (C_CONV - 1) // 2
    buf_ref[0:SUBLANES, :] = jnp.where(pos == 0, 0.0, xp_ref[...])
    buf_ref[SUBLANES:SUBLANES + tc, :] = x_ref[...]
    buf_ref[SUBLANES + tc:, :] = jnp.where(pos + tc == seq, 0.0, xn_ref[...])
    w = w_ref[...]
    y = jnp.zeros((tc, 3 * C_WIDTH), F32)
    for t in range(C_CONV):
        y = y + buf_ref[SUBLANES - halo + t:SUBLANES - halo + t + tc, :] * w[t:t + 1, :]
    y = y * jax.nn.sigmoid(y)
    for j in range(3 * C_HEADS):
        blk = y[:, j * LANES:(j + 1) * LANES]
        if j < 2 * C_HEADS:
            blk = blk * lax.rsqrt(jnp.sum(blk * blk, axis=-1, keepdims=True) + EPS)
        if j < C_HEADS:
            blk = blk * (C_HEAD_DIM ** -0.5)
        o_ref[:, j * LANES:(j + 1) * LANES] = blk


def _conv(x2, w, seq, tc):
    rows, width = x2.shape
    r = tc // SUBLANES
    nb8 = rows // SUBLANES
    return pl.pallas_call(
        functools.partial(_conv_kernel, tc=tc, seq=seq),
        grid=(rows // tc,),
        in_specs=[pl.BlockSpec((SUBLANES, width), lambda i: (jnp.maximum(i * r - 1, 0), 0)),
                  pl.BlockSpec((tc, width), lambda i: (i, 0)),
                  pl.BlockSpec((SUBLANES, width), lambda i: (jnp.minimum((i + 1) * r, nb8 - 1), 0)),
                  pl.BlockSpec((C_CONV, width), lambda i: (0, 0))],
        out_specs=pl.BlockSpec((tc, width), lambda i: (i, 0)),
        out_shape=jax.ShapeDtypeStruct((rows, width), F32),
        scratch_shapes=[pltpu.VMEM((tc + 2 * SUBLANES, width), F32)],
        compiler_params=_cparams(("parallel",)),
    )(x2, x2, x2, w)


HC = C_HEADS * C_CHUNK
CHUNK_SHIFT = 6


def _mm(a, b):
    return jnp.dot(a, b, preferred_element_type=F32)


def _split3(x):
    hi = x.astype(BF16)
    r = x - hi.astype(F32)
    mid = r.astype(BF16)
    return hi, mid, (r - mid.astype(F32)).astype(BF16)


def _block_diag(x, mask):
    return jnp.where(mask, jnp.concatenate([x] * C_HEADS, axis=0), jnp.zeros((), x.dtype))


def _delta_prep_kernel(qkv_ref, bg_ref, w_ref, u_ref, qd_ref, ktt_ref, att_ref, eg_ref, *, rb):
    c = C_CHUNK
    bg = bg_ref[...]
    g3 = _split3(bg)
    rr = lax.broadcasted_iota(jnp.int32, (rb, rb), 0)
    cc = lax.broadcasted_iota(jnp.int32, (rb, rb), 1)
    same = (rr >> CHUNK_SHIFT) == (cc >> CHUNK_SHIFT)
    tri = (jnp.where(same & (rr >= cc), 1.0, 0.0).astype(BF16),
           jnp.where(same & (rr <= cc), 1.0, 0.0).astype(BF16))
    gcum = [_mm(t, g3[0]) + _mm(t, g3[1]) + _mm(t, g3[2]) for t in tri]

    row = lax.broadcasted_iota(jnp.int32, (c, HC), 0)
    lm = lax.broadcasted_iota(jnp.int32, (c, HC), 1) & (c - 1)
    eye_cat = row == lm
    lane_lo = lax.broadcasted_iota(jnp.int32, (c, LANES), 1) < c
    bdmask = ((lax.broadcasted_iota(jnp.int32, (HC, HC), 0) >> CHUNK_SHIFT)
              == (lax.broadcasted_iota(jnp.int32, (HC, HC), 1) >> CHUNK_SHIFT))
    zeros_blk = jnp.zeros((c, LANES), BF16)
    heads = range(C_HEADS)

    def col(a, j):
        return jnp.broadcast_to(a[:, j:j + 1], (c, LANES))

    def spread(cols):
        return jnp.concatenate([jnp.where(lane_lo, cols[0], cols[1]),
                                jnp.where(lane_lo, cols[2], cols[3])], axis=1)

    amats = []
    rhss = []
    for ci in range(rb // c):
        rows = slice(ci * c, (ci + 1) * c)
        q = [qkv_ref[rows, h * LANES:(h + 1) * LANES] for h in heads]
        k = [qkv_ref[rows, C_WIDTH + h * LANES:C_WIDTH + (h + 1) * LANES] for h in heads]
        v = [qkv_ref[rows, 2 * C_WIDTH + h * LANES:2 * C_WIDTH + (h + 1) * LANES] for h in heads]
        kb = [t.astype(BF16) for t in k]
        lhs = jnp.concatenate([jnp.concatenate(kb, axis=1),
                               jnp.concatenate([t.astype(BF16) for t in q], axis=1)], axis=0)
        kmask = jnp.concatenate(
            [jnp.concatenate([kb[h] if j == h else zeros_blk for j in heads], axis=1) for h in heads],
            axis=0)
        gram = lax.dot_general(lhs, kmask, (((1,), (1,)), ((), ())), preferred_element_type=F32)
        kk = gram[:c]
        qk = gram[c:]
        beta_all = bg[rows]
        for d in range(2):
            gc = gcum[d][rows]
            off_b = d * C_HEADS
            off_g = 2 * C_HEADS + d * C_HEADS
            bcol = [col(beta_all, off_b + h) for h in heads]
            gcol = [col(gc, off_g + h) for h in heads]
            g_last = gc[c - 1:c, :] if d == 0 else gc[0:1, :]
            glcol = [jnp.broadcast_to(g_last[:, off_g + h:off_g + h + 1], (c, LANES)) for h in heads]
            gc_sp = spread(gcol)
            gc_row = jnp.sum(jnp.where(eye_cat, gc_sp, 0.0), axis=0, keepdims=True)
            incl = (row >= lm) if d == 0 else (row <= lm)
            strict = (row > lm) if d == 0 else (row < lm)
            decay = jnp.where(incl, jnp.exp(jnp.where(incl, gc_sp - gc_row, 0.0)), 0.0)
            amats.append(jnp.where(strict, spread(bcol) * kk * decay, 0.0))
            egc = [jnp.exp(gcol[h]) for h in heads]
            rhss.append(jnp.concatenate(
                [jnp.concatenate([v[h] * bcol[h], k[h] * (bcol[h] * egc[h])], axis=1) for h in heads],
                axis=0))
            qd_ref[ci, d] = jnp.concatenate([q[h] * egc[h] for h in heads], axis=0).astype(BF16)
            kt = jnp.concatenate([k[h] * jnp.exp(glcol[h] - gcol[h]) for h in heads], axis=0)
            ktt_ref[ci, d] = kt.T.astype(BF16)
            att_ref[ci, d] = jnp.where(incl, qk * decay, 0.0).astype(BF16)
            eg_ref[ci, d] = jnp.exp(jnp.concatenate([glcol[h][0:1, :] for h in heads], axis=0))

    units = range(len(amats))
    tms = [-a for a in amats]
    abs_ = [a.astype(BF16) for a in amats]
    pws = [_mm(abs_[i], _block_diag(abs_[i], bdmask)) for i in units]
    for _ in range(CHUNK_SHIFT - 2):
        pbs = [p.astype(BF16) for p in pws]
        both = [_mm(jnp.concatenate([tms[i].astype(BF16), pbs[i]], axis=0), _block_diag(pbs[i], bdmask))
                for i in units]
        tms = [tms[i] + pws[i] + both[i][:c] for i in units]
        pws = [both[i][c:] for i in units]
    last = [_mm(tms[i].astype(BF16), _block_diag(pws[i].astype(BF16), bdmask)) for i in units]
    tms = [tms[i] + pws[i] + last[i] for i in units]
    uws = [_mm(_block_diag(tms[i].astype(BF16), bdmask), rhss[i].astype(BF16)) for i in units]
    for i in units:
        uw = rhss[i] + uws[i]
        u_ref[i // 2, i % 2] = uw[:, :LANES]
        w_ref[i // 2, i % 2] = uw[:, LANES:].astype(BF16)


def _delta_prep(qkv2, bg2, rb):
    rows = qkv2.shape[0]
    nct = rows // C_CHUNK
    nc = rb // C_CHUNK
    blk = lambda i: (i, 0, 0, 0)
    shapes = [((HC, LANES), BF16), ((HC, LANES), F32), ((HC, LANES), BF16),
              ((C_HEAD_DIM, HC), BF16), ((C_CHUNK, HC), BF16), ((C_HEADS, LANES), F32)]
    return pl.pallas_call(
        functools.partial(_delta_prep_kernel, rb=rb),
        grid=(rows // rb,),
        in_specs=[pl.BlockSpec((rb, 3 * C_WIDTH), lambda i: (i, 0)),
                  pl.BlockSpec((rb, N_BA), lambda i: (i, 0))],
        out_specs=[pl.BlockSpec((nc, 2) + s, blk) for s, _ in shapes],
        out_shape=[jax.ShapeDtypeStruct((nct, 2) + s, dt) for s, dt in shapes],
        compiler_params=_cparams(("parallel",)),
    )(qkv2, bg2)


def _delta_scan_kernel(*refs, cb, bsz):
    ins = (refs[0:6], refs[6:12])
    outs = refs[12:14]
    state_ref = refs[14]
    c = C_CHUNK

    @pl.when(pl.program_id(0) == 0)
    def _():
        state_ref[...] = jnp.zeros_like(state_ref)

    bdmask = ((lax.broadcasted_iota(jnp.int32, (HC, HC), 0) >> CHUNK_SHIFT)
              == (lax.broadcasted_iota(jnp.int32, (HC, HC), 1) >> CHUNK_SHIFT))
    lane_head = lax.broadcasted_iota(jnp.int32, (1, HC), 1) >> CHUNK_SHIFT
    heads = range(C_HEADS)
    for step in range(cb):
        for b in range(bsz):
            for d in range(2):
                w_ref, u_ref, qd_ref, ktt_ref, att_ref, eg_ref = ins[d]
                ci = step if d == 0 else cb - 1 - step
                w = w_ref[b, ci]
                qd = qd_ref[b, ci]
                ktt = ktt_ref[b, ci]
                eg = eg_ref[b, ci]
                base = (b * 2 + d) * C_HEADS
                state = [state_ref[base + h] for h in heads]
                prod = [_mm(jnp.concatenate([w[h * c:(h + 1) * c], qd[h * c:(h + 1) * c]], axis=0),
                            state[h].astype(BF16)) for h in heads]
                ws = jnp.concatenate([p[:c] for p in prod], axis=0)
                qs = jnp.concatenate([p[c:] for p in prod], axis=0)
                vnb = (u_ref[b, ci] - ws).astype(BF16)
                o = qs + _mm(_block_diag(att_ref[b, ci], bdmask), vnb)
                for h in heads:
                    kth = jnp.where(lane_head == h, ktt, jnp.zeros((), ktt.dtype))
                    state_ref[base + h] = state[h] * eg[h:h + 1, :] + _mm(kth, vnb)
                    outs[d][b, ci * c:(ci + 1) * c, h * LANES:(h + 1) * LANES] = o[h * c:(h + 1) * c]


def _delta_scan(prep, bsz, seq, cb):
    nch = seq // C_CHUNK
    nblk = nch // cb
    arrs = [a.reshape((bsz, nch) + a.shape[1:]) for a in prep]

    def specs(d):
        idx = (lambda n: (0, n, d, 0, 0)) if d == 0 else (lambda n: (0, nblk - 1 - n, d, 0, 0))
        return [pl.BlockSpec((bsz, cb, None) + a.shape[3:], idx) for a in arrs]

    out = jax.ShapeDtypeStruct((bsz, seq, C_WIDTH), F32)
    return pl.pallas_call(
        functools.partial(_delta_scan_kernel, cb=cb, bsz=bsz),
        grid=(nblk,),
        in_specs=specs(0) + specs(1),
        out_specs=[pl.BlockSpec((bsz, cb * C_CHUNK, C_WIDTH), lambda n: (0, n, 0)),
                   pl.BlockSpec((bsz, cb * C_CHUNK, C_WIDTH), lambda n: (0, nblk - 1 - n, 0))],
        out_shape=[out, out],
        scratch_shapes=[pltpu.VMEM((bsz * 2 * C_HEADS, C_HEAD_DIM, C_HEAD_DIM), F32)],
        compiler_params=_cparams(("arbitrary",)),
    )(*arrs, *arrs)


def _merge_kernel(x_ref, oa_ref, ob_ref, ocf_ref, ocb_ref, zc_ref, gates_ref, cg_ref, wa_ref, wb_ref, wc_ref,
                  wo_ref, fg_ref, o_ref, *, final):
    oc = ocf_ref[...] + ocb_ref[...]
    cg = cg_ref[...]
    parts = []
    for h in range(C_HEADS):
        blk = oc[:, h * LANES:(h + 1) * LANES]
        blk = blk * lax.rsqrt(jnp.mean(blk * blk, axis=-1, keepdims=True) + EPS) * cg
        parts.append((blk * zc_ref[:, h * LANES:(h + 1) * LANES].astype(F32)).astype(BF16))
    ocn = jnp.concatenate(parts, axis=1)
    ya = jnp.dot(oa_ref[...], wa_ref[...], preferred_element_type=F32)
    yb = jnp.dot(ob_ref[...], wb_ref[...], preferred_element_type=F32)
    yc = jnp.dot(ocn, wc_ref[...], preferred_element_type=F32)
    d = D_MODEL
    merged = (gates_ref[:, 0:d].astype(F32) * ya + gates_ref[:, d:2 * d].astype(F32) * yb
              + gates_ref[:, 2 * d:3 * d].astype(F32) * yc)
    x = x_ref[...] + jnp.dot(merged.astype(BF16), wo_ref[...], preferred_element_type=F32)
    if final:
        x = x * lax.rsqrt(jnp.mean(x * x, axis=-1, keepdims=True) + EPS) * fg_ref[...]
    o_ref[...] = x


def _merge(x2, oa, ob, ocf, ocb, zc, gates, cg, wa, wb, wc, wo, fg, tm, final):
    rows = x2.shape[0]
    row = lambda i: (i, 0)
    const = lambda i: (0, 0)
    return pl.pallas_call(
        functools.partial(_merge_kernel, final=final),
        grid=(rows // tm,),
        in_specs=[pl.BlockSpec((tm, D_MODEL), row),
                  pl.BlockSpec((tm, A_WIDTH), row), pl.BlockSpec((tm, B_WIDTH), row),
                  pl.BlockSpec((tm, C_WIDTH), row), pl.BlockSpec((tm, C_WIDTH), row),
                  pl.BlockSpec((tm, C_WIDTH), row), pl.BlockSpec((tm, N_BRANCH * D_MODEL), row),
                  pl.BlockSpec((1, LANES), const),
                  pl.BlockSpec((A_WIDTH, D_MODEL), const), pl.BlockSpec((B_WIDTH, D_MODEL), const),
                  pl.BlockSpec((C_WIDTH, D_MODEL), const), pl.BlockSpec((D_MODEL, D_MODEL), const),
                  pl.BlockSpec((1, D_MODEL), const)],
        out_specs=pl.BlockSpec((tm, D_MODEL), row),
        out_shape=jax.ShapeDtypeStruct((rows, D_MODEL), F32),
        compiler_params=_cparams(("parallel",)),
    )(x2, oa, ob, ocf, ocb, zc, gates, cg, wa, wb, wc, wo, fg)


def _rope_tables(seq):
    inv = 1.0 / (ROPE_THETA ** (jnp.arange(0, HEAD_DIM, 2, dtype=F32) / HEAD_DIM))
    ang = jnp.arange(seq, dtype=F32)[:, None] * inv[None, :]
    cos = jnp.tile(jnp.cos(ang), (1, LANES // (HEAD_DIM // 2)))
    sin = jnp.tile(jnp.concatenate([-jnp.sin(ang), jnp.sin(ang)], axis=1), (1, LANES // HEAD_DIM))
    return cos, sin


def kernel(x, norm_g, w_in, a_sink, b_lambda, b_subln_g, c_conv_w, c_a_log, c_dt_bias, c_norm_g,
           w_bo_a, w_bo_b, w_bo_c, w_out, final_g):
    bsz, seq, _ = x.shape
    depth = w_in.shape[0]
    rows = bsz * seq
    tm = min(256, seq)
    tq_a = min(256, seq)
    tq_b = min(512, seq)
    tk_b = min(1024, seq)
    cos, sin = _rope_tables(seq)

    def pair_heads(t, axis):
        shp = t.shape
        t = t.reshape(shp[:axis] + (A_KV_HEADS, A_Q_HEADS // A_KV_HEADS, HEAD_DIM) + shp[axis + 1:])
        return jnp.swapaxes(t, axis, axis + 1).reshape(shp)

    def sec(i):
        return w_in[:, :, _OFF[i]:_OFF[i + 1]]

    qscale = HEAD_DIM ** -0.5
    w_main = jnp.concatenate(
        [pair_heads(sec(0), 2) * qscale, sec(1), sec(2), pair_heads(sec(3), 2), sec(4) * qscale,
         sec(5), sec(6), sec(7), sec(8), sec(9), sec(12)], axis=2).astype(BF16)
    w_ba = w_in[:, :, _OFF[10]:_OFF[12]].astype(BF16)
    dec = jnp.stack([jnp.concatenate([jnp.zeros((depth, 2 * C_HEADS), F32),
                                      -jnp.exp(c_a_log.astype(F32)).reshape(depth, 2 * C_HEADS)], axis=1),
                     jnp.concatenate([jnp.zeros((depth, 2 * C_HEADS), F32),
                                      c_dt_bias.astype(F32).reshape(depth, 2 * C_HEADS)], axis=1)], axis=1)
    wa = pair_heads(w_bo_a, 1).astype(BF16)
    wb = w_bo_b.astype(BF16)
    wc = w_bo_c.astype(BF16)
    wo = w_out.astype(BF16)

    x2 = x.reshape(rows, D_MODEL)
    for l in range(depth):
        (qa, ka, va, za, qb, kb, vb, zb, cqkv, zc, gates, bg) = _inproj(
            x2, norm_g[l][None], w_main[l], w_ba[l], cos, sin, dec[l], seq, tm)

        r3 = lambda t: t.reshape(bsz, seq, t.shape[-1])
        oa = _attn_a(a_sink[l].astype(F32), r3(qa), r3(ka), r3(va), r3(za), tq_a)

        lam_init = 0.8 - 0.6 * math.exp(-0.3 * l)
        bl = b_lambda[l].astype(F32)
        lam = jnp.exp(jnp.sum(bl[0] * bl[1])) - jnp.exp(jnp.sum(bl[2] * bl[3])) + lam_init
        lam_s = jnp.stack([lam, jnp.asarray(1.0 - lam_init, F32)])
        ob = _attn_b(lam_s, r3(qb), r3(kb), r3(vb), r3(zb), b_subln_g[l][None].astype(F32), tq_b, tk_b)

        qkv = _conv(cqkv, c_conv_w[l].astype(F32), seq, min(256, seq))
        prep = _delta_prep(qkv, bg, min(256, seq))
        ocf, ocb = _delta_scan(prep, bsz, seq, 2)

        x2 = _merge(x2, oa.reshape(rows, A_WIDTH), ob.reshape(rows, B_WIDTH),
                    ocf.reshape(rows, C_WIDTH), ocb.reshape(rows, C_WIDTH), zc, gates,
                    c_norm_g[l][None].astype(F32), wa[l], wb[l], wc[l], wo[l],
                    final_g[None].astype(F32), tm, l == depth - 1)
    return x2.reshape(bsz, seq, D_MODEL)
```

```python
import functools
import math

import numpy as np
import jax
import jax.numpy as jnp
from jax import lax
from jax.experimental import pallas as pl
from jax.experimental.pallas import tpu as pltpu

F32 = jnp.float32
BF16 = jnp.bfloat16

D_MODEL = 1024
HEAD_DIM = 64
ROPE_THETA = 10000.0
EPS = 1e-6
A_Q_HEADS = 8
A_KV_HEADS = 2
A_WIDTH = A_Q_HEADS * HEAD_DIM
A_KV_WIDTH = A_KV_HEADS * HEAD_DIM
WINDOW = 128
B_HEADS = 4
B_V_DIM = 2 * HEAD_DIM
B_WIDTH = B_HEADS * B_V_DIM
C_HEADS = 4
C_HEAD_DIM = 128
C_WIDTH = C_HEADS * C_HEAD_DIM
C_CONV = 5
C_CHUNK = 64
N_BRANCH = 3
LOG2E = math.log2(math.e)
LANES = 128
SUBLANES = 8
MXU_N = 256
VMEM_LIMIT = 56 * 1024 * 1024

IN_SPLITS = (A_WIDTH, A_KV_WIDTH, A_KV_WIDTH, A_WIDTH,
             B_WIDTH, B_WIDTH, B_WIDTH, B_WIDTH,
             3 * C_WIDTH, C_WIDTH, 2 * C_HEADS, 2 * C_HEADS,
             N_BRANCH * D_MODEL)
_OFF = [0] + [int(o) for o in np.cumsum(IN_SPLITS)]

_PERM_A = np.concatenate([np.concatenate([np.arange(64 * b, 64 * b + 64),
                                          np.arange(64 * (4 + b), 64 * (4 + b) + 64)])
                          for b in range(4)])
_SEC = {}
_cols = []


def _add_section(name, idx):
    _SEC[name] = (sum(len(c) for c in _cols), len(idx))
    _cols.append(np.asarray(idx))


_add_section('aq', _OFF[0] + _PERM_A)
_add_section('ak', np.arange(_OFF[1], _OFF[2]))
_add_section('av', np.arange(_OFF[2], _OFF[3]))
_add_section('az', _OFF[3] + _PERM_A)
_add_section('bq', np.arange(_OFF[4], _OFF[5]))
_add_section('bk', np.arange(_OFF[5], _OFF[6]))
_add_section('bv', np.arange(_OFF[6], _OFF[7]))
_add_section('bz', np.arange(_OFF[7], _OFF[8]))
_add_section('cqkv', np.arange(_OFF[8], _OFF[9]))
_add_section('cz', np.arange(_OFF[9], _OFF[10]))
_add_section('gates', np.arange(_OFF[12], _OFF[13]))
_add_section('cba', np.arange(_OFF[10], _OFF[12]))
_COL_IDX = np.concatenate(_cols)
N_BA = 4 * C_HEADS
N_PROJ = len(_COL_IDX) - N_BA


def _cparams(sem):
    return pltpu.CompilerParams(dimension_semantics=sem, vmem_limit_bytes=VMEM_LIMIT)


def _rope(t, cos, sin_signed, first_half):
    swapped = jnp.where(first_half, pltpu.roll(t, LANES - 32, 1), pltpu.roll(t, 32, 1))
    return t * cos + swapped * sin_signed


def _softplus(t):
    return jnp.maximum(t, 0.0) + jnp.log(1.0 + jnp.exp(-jnp.abs(t)))


def _inproj_kernel(x_ref, g_ref, w_ref, wba_ref, cos_ref, sin_ref, dec_ref,
                   qa_ref, ka_ref, va_ref, za_ref, qb_ref, kb_ref, vb_ref, zb_ref,
                   cqkv_ref, zc_ref, gates_ref, bg_ref):
    x = x_ref[...]
    h = x * lax.rsqrt(jnp.mean(x * x, axis=-1, keepdims=True) + EPS) * g_ref[...]
    hb = h.astype(BF16)
    cos = cos_ref[...]
    sin = sin_ref[...]
    first_half = (lax.broadcasted_iota(jnp.int32, cos.shape, 1) & 32) == 0

    def rope(t):
        return _rope(t, cos, sin, first_half)

    def silu(t):
        return t * jax.nn.sigmoid(t)

    def rope_q(t):
        return rope(t) * LOG2E

    epilogues = (('aq', qa_ref, rope_q), ('ak', ka_ref, rope), ('av', va_ref, None), ('az', za_ref, silu),
                 ('bq', qb_ref, rope_q), ('bk', kb_ref, rope), ('bv', vb_ref, None),
                 ('bz', zb_ref, silu), ('cqkv', cqkv_ref, None), ('cz', zc_ref, silu),
                 ('gates', gates_ref, jax.nn.sigmoid))
    slabs = []
    for name, ref, fn in epilogues:
        assert _SEC[name][0] == len(slabs) * LANES
        slabs += [(ref, j, fn) for j in range(_SEC[name][1] // LANES)]
    for pair in range(len(slabs) // 2):
        acc = jnp.dot(hb, w_ref[:, pair * MXU_N:(pair + 1) * MXU_N], preferred_element_type=F32)
        for half in range(2):
            ref, j, fn = slabs[2 * pair + half]
            t = acc[:, half * LANES:(half + 1) * LANES]
            ref[:, j * LANES:(j + 1) * LANES] = (t if fn is None else fn(t)).astype(ref.dtype)

    ba = jnp.dot(hb, wba_ref[...], preferred_element_type=F32)
    is_beta = lax.broadcasted_iota(jnp.int32, ba.shape, 1) < 2 * C_HEADS
    dec = dec_ref[...]
    bg_ref[...] = jnp.where(is_beta, jax.nn.sigmoid(ba), dec[0:1, :] * _softplus(ba + dec[1:2, :]))


def _inproj(x2, g, w, wba, cos, sin, dec, seq, tm):
    rows = x2.shape[0]
    nseq = seq // tm
    row = lambda i: (i, 0)
    const = lambda i: (0, 0)
    single = pl.Buffered(1)
    out_widths = [('aq', BF16), ('ak', BF16), ('av', BF16), ('az', BF16), ('bq', BF16), ('bk', BF16),
                  ('bv', BF16), ('bz', BF16), ('cqkv', F32), ('cz', BF16), ('gates', BF16)]
    out_shape = [jax.ShapeDtypeStruct((rows, _SEC[n][1]), dt) for n, dt in out_widths]
    out_specs = [pl.BlockSpec((tm, _SEC[n][1]), row) for n, _ in out_widths]
    out_shape += [jax.ShapeDtypeStruct((rows, N_BA), F32)]
    out_specs += [pl.BlockSpec((tm, N_BA), row)]
    return pl.pallas_call(
        _inproj_kernel,
        grid=(rows // tm,),
        in_specs=[pl.BlockSpec((tm, D_MODEL), row),
                  pl.BlockSpec((1, D_MODEL), const),
                  pl.BlockSpec((D_MODEL, N_PROJ), const, pipeline_mode=single),
                  pl.BlockSpec((D_MODEL, N_BA), const),
                  pl.BlockSpec((tm, LANES), lambda i: (i % nseq, 0)),
                  pl.BlockSpec((tm, LANES), lambda i: (i % nseq, 0)),
                  pl.BlockSpec((2, N_BA), const)],
        out_specs=out_specs,
        out_shape=out_shape,
        compiler_params=_cparams(("parallel",)),
    )(x2, g, w, wba, cos, sin, dec)


def _attn_a_kernel(sink_ref, q_ref, kp_ref, k_ref, kn_ref, vp_ref, v_ref, vn_ref, z_ref, o_ref, *, tq, seq):
    i = pl.program_id(1)
    kcat = jnp.concatenate([kp_ref[0], k_ref[0], kn_ref[0]], axis=0)
    vcat = jnp.concatenate([vp_ref[0], v_ref[0], vn_ref[0]], axis=0)
    nk = tq + 2 * WINDOW
    lane = lax.broadcasted_iota(jnp.int32, (1, LANES), 1)
    low = lane < HEAD_DIM
    one = jnp.ones_like(vcat)
    v_lo = jnp.where(low, vcat, one)
    v_hi = jnp.where(low, one, vcat)
    r = lax.broadcasted_iota(jnp.int32, (tq, nk), 0)
    c = lax.broadcasted_iota(jnp.int32, (tq, nk), 1)
    kabs = i * tq - WINDOW + c
    valid = (jnp.abs(c - WINDOW - r) <= WINDOW) & (kabs >= 0) & (kabs < seq)

    def probs(qm, sink):
        s = lax.dot_general(qm, kcat, (((1,), (1,)), ((), ())), preferred_element_type=F32)
        s = jnp.where(valid, s, -1e30)
        m = jnp.maximum(jnp.max(s, axis=-1, keepdims=True), sink)
        return jnp.exp2(s - m).astype(BF16), jnp.exp2(sink - m)

    half = A_Q_HEADS // 2
    for b in range(half):
        q = q_ref[0, :, b * LANES:(b + 1) * LANES]
        p0, e0 = probs(jnp.where(low, q, jnp.zeros_like(q)), sink_ref[b] * LOG2E)
        p1, e1 = probs(jnp.where(low, jnp.zeros_like(q), q), sink_ref[half + b] * LOG2E)
        r0 = jnp.dot(p0, v_lo, preferred_element_type=F32)
        r1 = jnp.dot(p1, v_hi, preferred_element_type=F32)
        num = jnp.where(low, r0, r1)
        den = (jnp.where(low, pltpu.roll(r0, HEAD_DIM, 1), pltpu.roll(r1, HEAD_DIM, 1))
               + jnp.where(low, e0, e1))
        o_ref[0, :, b * LANES:(b + 1) * LANES] = (
            num / den * z_ref[0, :, b * LANES:(b + 1) * LANES].astype(F32)).astype(BF16)


def _attn_a(sink, q, k, v, z, tq):
    bsz, seq, _ = q.shape
    r = tq // WINDOW
    nwb = seq // WINDOW
    main = lambda b, i: (b, i, 0)
    prev = lambda b, i: (b, jnp.maximum(i * r - 1, 0), 0)
    nxt = lambda b, i: (b, jnp.minimum((i + 1) * r, nwb - 1), 0)
    kv_specs = [pl.BlockSpec((1, WINDOW, LANES), prev), pl.BlockSpec((1, tq, LANES), main),
                pl.BlockSpec((1, WINDOW, LANES), nxt)]
    return pl.pallas_call(
        functools.partial(_attn_a_kernel, tq=tq, seq=seq),
        grid=(bsz, seq // tq),
        in_specs=[pl.BlockSpec(memory_space=pltpu.SMEM), pl.BlockSpec((1, tq, A_WIDTH), main)]
                 + kv_specs + kv_specs + [pl.BlockSpec((1, tq, A_WIDTH), main)],
        out_specs=pl.BlockSpec((1, tq, A_WIDTH), main),
        out_shape=jax.ShapeDtypeStruct((bsz, seq, A_WIDTH), BF16),
        compiler_params=_cparams(("parallel", "parallel")),
    )(sink, q, k, k, k, v, v, v, z)


def _attn_b_kernel(lam_ref, q_ref, k_ref, v_ref, z_ref, g_ref, o_ref, acc_ref, m_ref, s_ref,
                   *, tq, tk, seq):
    q = q_ref[0]
    low = lax.broadcasted_iota(jnp.int32, (1, LANES), 1) < HEAD_DIM
    zero = jnp.zeros_like(q)
    q2 = jnp.concatenate([jnp.where(low, q, zero), jnp.where(low, zero, q)], axis=0)
    acc_ref[...] = jnp.zeros_like(acc_ref)
    m_ref[...] = jnp.full_like(m_ref, -jnp.inf)
    ones = jnp.ones((MXU_N, LANES), BF16)
    nk = seq // tk

    def scores(j):
        return lax.dot_general(q2, k_ref[0, j * tk:(j + 1) * tk, :], (((1,), (1,)), ((), ())),
                               preferred_element_type=F32)

    def update(s_ref, j):
        m_old = m_ref[...]
        m_new = jnp.maximum(m_old, jnp.max(s_ref[...], axis=-1, keepdims=True))
        m_ref[...] = m_new
        pv = None
        for kb in range(tk // MXU_N):
            lo = kb * MXU_N
            p = jnp.concatenate(
                [jnp.exp2(s_ref[:, lo + c * LANES:lo + (c + 1) * LANES] - m_ref[...]).astype(BF16)
                 for c in range(MXU_N // LANES)], axis=1)
            vc = jnp.concatenate([v_ref[0, j * tk + lo:j * tk + lo + MXU_N, :], ones], axis=1)
            d = jnp.dot(p, vc, preferred_element_type=F32)
            pv = d if pv is None else pv + d
        alpha = jnp.exp2(m_old - m_new)
        acc_ref[...] = acc_ref[...] * jnp.concatenate([alpha, alpha], axis=1) + pv

    nbuf = s_ref.shape[0]
    s_ref[0] = scores(0)
    for j in range(nk):
        if j + 1 < nk:
            s_ref[(j + 1) % nbuf] = scores(j + 1)
        update(s_ref.at[j % nbuf], j)
    acc = acc_ref[...]
    o1 = acc[:tq, :LANES] / acc[:tq, LANES:]
    o2 = acc[tq:, :LANES] / acc[tq:, LANES:]
    o = o1 - lam_ref[0] * o2
    o = o * lax.rsqrt(jnp.mean(o * o, axis=-1, keepdims=True) + EPS) * g_ref[...] * lam_ref[1]
    o_ref[0] = (o * z_ref[0].astype(F32)).astype(BF16)


def _attn_b(lam, q, k, v, z, g, tq, tk):
    bsz, seq, _ = q.shape
    assert seq % tk == 0 and seq % tq == 0
    qmap = lambda b, h, i: (b, i, h)
    kmap = lambda b, h, i: (b, 0, h)
    return pl.pallas_call(
        functools.partial(_attn_b_kernel, tq=tq, tk=tk, seq=seq),
        grid=(bsz, B_HEADS, seq // tq),
        in_specs=[pl.BlockSpec(memory_space=pltpu.SMEM),
                  pl.BlockSpec((1, tq, LANES), qmap),
                  pl.BlockSpec((1, seq, LANES), kmap),
                  pl.BlockSpec((1, seq, LANES), kmap),
                  pl.BlockSpec((1, tq, LANES), qmap),
                  pl.BlockSpec((1, LANES), lambda b, h, i: (0, 0))],
        out_specs=pl.BlockSpec((1, tq, LANES), qmap),
        out_shape=jax.ShapeDtypeStruct((bsz, seq, B_WIDTH), BF16),
        scratch_shapes=[pltpu.VMEM((2 * tq, 2 * LANES), F32), pltpu.VMEM((2 * tq, LANES), F32),
                        pltpu.VMEM((2, 2 * tq, tk), F32)],
        compiler_params=_cparams(("parallel", "parallel", "parallel")),
    )(lam, q, k, v, z, g)


def _short_conv(xp_ref, x_ref, xn_ref, w_ref, buf_ref, qkv_ref, *, tc, seq):
    pos = (pl.program_id(0) * tc) % seq
    halo = (C_CONV - 1) // 2
    buf_ref[0:SUBLANES, :] = jnp.where(pos == 0, 0.0, xp_ref[...])
    buf_ref[SUBLANES:SUBLANES + tc, :] = x_ref[...]
    buf_ref[SUBLANES + tc:, :] = jnp.where(pos + tc == seq, 0.0, xn_ref[...])
    for j in range(3 * C_HEADS):
        cols = slice(j * LANES, (j + 1) * LANES)
        y = jnp.zeros((tc, LANES), F32)
        for t in range(C_CONV):
            y = y + buf_ref[SUBLANES - halo + t:SUBLANES - halo + t + tc, cols] * w_ref[t:t + 1, cols]
        y = y * jax.nn.sigmoid(y)
        if j < 2 * C_HEADS:
            y = y * lax.rsqrt(jnp.sum(y * y, axis=-1, keepdims=True) + EPS)
        if j < C_HEADS:
            y = y * (C_HEAD_DIM ** -0.5)
        qkv_ref[:, cols] = y


HC = C_HEADS * C_CHUNK
CHUNK_SHIFT = 6


def _mm(a, b):
    return jnp.dot(a, b, preferred_element_type=F32)


def _split3(x):
    hi = x.astype(BF16)
    r = x - hi.astype(F32)
    mid = r.astype(BF16)
    return hi, mid, (r - mid.astype(F32)).astype(BF16)


def _block_diag(x, mask):
    return jnp.where(mask, jnp.concatenate([x] * C_HEADS, axis=0), jnp.zeros((), x.dtype))


def _delta_prep_kernel(xp_ref, x_ref, xn_ref, cw_ref, bg_ref, w_ref, u_ref, qd_ref, ktt_ref, att_ref, eg_ref,
                       buf_ref, qkv_ref, *, rb, seq):
    c = C_CHUNK
    _short_conv(xp_ref, x_ref, xn_ref, cw_ref, buf_ref, qkv_ref, tc=rb, seq=seq)
    bg = bg_ref[...]
    g3 = _split3(bg)
    rr = lax.broadcasted_iota(jnp.int32, (rb, rb), 0)
    cc = lax.broadcasted_iota(jnp.int32, (rb, rb), 1)
    same = (rr >> CHUNK_SHIFT) == (cc >> CHUNK_SHIFT)
    tri = (jnp.where(same & (rr >= cc), 1.0, 0.0).astype(BF16),
           jnp.where(same & (rr <= cc), 1.0, 0.0).astype(BF16))
    gcum = [_mm(t, g3[0]) + _mm(t, g3[1]) + _mm(t, g3[2]) for t in tri]

    row = lax.broadcasted_iota(jnp.int32, (c, HC), 0)
    lm = lax.broadcasted_iota(jnp.int32, (c, HC), 1) & (c - 1)
    eye_cat = row == lm
    lane_lo = lax.broadcasted_iota(jnp.int32, (c, LANES), 1) < c
    bdmask = ((lax.broadcasted_iota(jnp.int32, (HC, HC), 0) >> CHUNK_SHIFT)
              == (lax.broadcasted_iota(jnp.int32, (HC, HC), 1) >> CHUNK_SHIFT))
    zeros_blk = jnp.zeros((c, LANES), BF16)
    heads = range(C_HEADS)

    def col(a, j):
        return jnp.broadcast_to(a[:, j:j + 1], (c, LANES))

    def spread(cols):
        return jnp.concatenate([jnp.where(lane_lo, cols[0], cols[1]),
                                jnp.where(lane_lo, cols[2], cols[3])], axis=1)

    amats = []
    rhss = []
    for ci in range(rb // c):
        rows = slice(ci * c, (ci + 1) * c)
        q = [qkv_ref[rows, h * LANES:(h + 1) * LANES] for h in heads]
        k = [qkv_ref[rows, C_WIDTH + h * LANES:C_WIDTH + (h + 1) * LANES] for h in heads]
        v = [qkv_ref[rows, 2 * C_WIDTH + h * LANES:2 * C_WIDTH + (h + 1) * LANES] for h in heads]
        kb = [t.astype(BF16) for t in k]
        lhs = jnp.concatenate([jnp.concatenate(kb, axis=1),
                               jnp.concatenate([t.astype(BF16) for t in q], axis=1)], axis=0)
        kmask = jnp.concatenate(
            [jnp.concatenate([kb[h] if j == h else zeros_blk for j in heads], axis=1) for h in heads],
            axis=0)
        gram = lax.dot_general(lhs, kmask, (((1,), (1,)), ((), ())), preferred_element_type=F32)
        kk = gram[:c]
        qk = gram[c:]
        beta_all = bg[rows]
        for d in range(2):
            gc = gcum[d][rows]
            off_b = d * C_HEADS
            off_g = 2 * C_HEADS + d * C_HEADS
            bcol = [col(beta_all, off_b + h) for h in heads]
            gcol = [col(gc, off_g + h) for h in heads]
            g_last = gc[c - 1:c, :] if d == 0 else gc[0:1, :]
            glcol = [jnp.broadcast_to(g_last[:, off_g + h:off_g + h + 1], (c, LANES)) for h in heads]
            gc_sp = spread(gcol)
            gc_row = jnp.sum(jnp.where(eye_cat, gc_sp, 0.0), axis=0, keepdims=True)
            incl = (row >= lm) if d == 0 else (row <= lm)
            strict = (row > lm) if d == 0 else (row < lm)
            decay = jnp.where(incl, jnp.exp(jnp.where(incl, gc_sp - gc_row, 0.0)), 0.0)
            amats.append(jnp.where(strict, spread(bcol) * kk * decay, 0.0))
            egc = [jnp.exp(gcol[h]) for h in heads]
            rhss.append(jnp.concatenate(
                [jnp.concatenate([v[h] * bcol[h], k[h] * (bcol[h] * egc[h])], axis=1) for h in heads],
                axis=0))
            qd_ref[ci, d] = jnp.concatenate([q[h] * egc[h] for h in heads], axis=0).astype(BF16)
            kt = jnp.concatenate([k[h] * jnp.exp(glcol[h] - gcol[h]) for h in heads], axis=0)
            ktt_ref[ci, d] = kt.T.astype(BF16)
            att_ref[ci, d] = jnp.where(incl, qk * decay, 0.0).astype(BF16)
            eg_ref[ci, d] = jnp.exp(jnp.concatenate([glcol[h][0:1, :] for h in heads], axis=0))

    units = range(len(amats))
    tms = [-a for a in amats]
    abs_ = [a.astype(BF16) for a in amats]
    pws = [_mm(abs_[i], _block_diag(abs_[i], bdmask)) for i in units]
    for _ in range(CHUNK_SHIFT - 2):
        pbs = [p.astype(BF16) for p in pws]
        both = [_mm(jnp.concatenate([tms[i].astype(BF16), pbs[i]], axis=0), _block_diag(pbs[i], bdmask))
                for i in units]
        tms = [tms[i] + pws[i] + both[i][:c] for i in units]
        pws = [both[i][c:] for i in units]
    last = [_mm(tms[i].astype(BF16), _block_diag(pws[i].astype(BF16), bdmask)) for i in units]
    tms = [tms[i] + pws[i] + last[i] for i in units]
    uws = [_mm(_block_diag(tms[i].astype(BF16), bdmask), rhss[i].astype(BF16)) for i in units]
    for i in units:
        uw = rhss[i] + uws[i]
        u_ref[i // 2, i % 2] = uw[:, :LANES]
        w_ref[i // 2, i % 2] = uw[:, LANES:].astype(BF16)


def _delta_prep(x2, conv_w, bg2, seq, rb):
    rows, width = x2.shape
    nct = rows // C_CHUNK
    nc = rb // C_CHUNK
    r = rb // SUBLANES
    nb8 = rows // SUBLANES
    blk = lambda i: (i, 0, 0, 0)
    shapes = [((HC, LANES), BF16), ((HC, LANES), F32), ((HC, LANES), BF16),
              ((C_HEAD_DIM, HC), BF16), ((C_CHUNK, HC), BF16), ((C_HEADS, LANES), F32)]
    return pl.pallas_call(
        functools.partial(_delta_prep_kernel, rb=rb, seq=seq),
        grid=(rows // rb,),
        in_specs=[pl.BlockSpec((SUBLANES, width), lambda i: (jnp.maximum(i * r - 1, 0), 0)),
                  pl.BlockSpec((rb, width), lambda i: (i, 0)),
                  pl.BlockSpec((SUBLANES, width), lambda i: (jnp.minimum((i + 1) * r, nb8 - 1), 0)),
                  pl.BlockSpec((C_CONV, width), lambda i: (0, 0)),
                  pl.BlockSpec((rb, N_BA), lambda i: (i, 0))],
        out_specs=[pl.BlockSpec((nc, 2) + s, blk) for s, _ in shapes],
        out_shape=[jax.ShapeDtypeStruct((nct, 2) + s, dt) for s, dt in shapes],
        scratch_shapes=[pltpu.VMEM((rb + 2 * SUBLANES, width), F32), pltpu.VMEM((rb, width), F32)],
        compiler_params=_cparams(("parallel",)),
    )(x2, x2, x2, conv_w, bg2)


def _delta_scan_kernel(*refs, cb, bsz):
    ins = (refs[0:6], refs[6:12])
    outs = refs[12:14]
    state_ref = refs[14]
    c = C_CHUNK

    @pl.when(pl.program_id(0) == 0)
    def _():
        state_ref[...] = jnp.zeros_like(state_ref)

    bdmask = ((lax.broadcasted_iota(jnp.int32, (HC, HC), 0) >> CHUNK_SHIFT)
              == (lax.broadcasted_iota(jnp.int32, (HC, HC), 1) >> CHUNK_SHIFT))
    lane_head = lax.broadcasted_iota(jnp.int32, (1, HC), 1) >> CHUNK_SHIFT
    heads = range(C_HEADS)
    chains = [(b, d) for b in range(bsz) for d in range(2)]
    for step in range(cb):
        ci = {d: step if d == 0 else cb - 1 - step for d in range(2)}
        state, prod = {}, {}
        for b, d in chains:
            w = ins[d][0][b, ci[d]]
            qd = ins[d][2][b, ci[d]]
            base = (b * 2 + d) * C_HEADS
            state[b, d] = [state_ref[base + h] for h in heads]
            prod[b, d] = [_mm(jnp.concatenate([w[h * c:(h + 1) * c], qd[h * c:(h + 1) * c]], axis=0),
                              state[b, d][h].astype(BF16)) for h in heads]
        vnb, att_v = {}, {}
        for b, d in chains:
            ws = jnp.concatenate([p[:c] for p in prod[b, d]], axis=0)
            vnb[b, d] = (ins[d][1][b, ci[d]] - ws).astype(BF16)
            att_v[b, d] = _mm(_block_diag(ins[d][4][b, ci[d]], bdmask), vnb[b, d])
        for b, d in chains:
            ktt = ins[d][3][b, ci[d]]
            eg = ins[d][5][b, ci[d]]
            base = (b * 2 + d) * C_HEADS
            for h in heads:
                kth = jnp.where(lane_head == h, ktt, jnp.zeros((), ktt.dtype))
                state_ref[base + h] = state[b, d][h] * eg[h:h + 1, :] + _mm(kth, vnb[b, d])
                outs[d][b, ci[d] * c:(ci[d] + 1) * c, h * LANES:(h + 1) * LANES] = (
                    prod[b, d][h][c:] + att_v[b, d][h * c:(h + 1) * c])


def _delta_scan(prep, bsz, seq, cb):
    nch = seq // C_CHUNK
    nblk = nch // cb
    arrs = [a.reshape((bsz, nch) + a.shape[1:]) for a in prep]

    def specs(d):
        idx = (lambda n: (0, n, d, 0, 0)) if d == 0 else (lambda n: (0, nblk - 1 - n, d, 0, 0))
        return [pl.BlockSpec((bsz, cb, None) + a.shape[3:], idx) for a in arrs]

    out = jax.ShapeDtypeStruct((bsz, seq, C_WIDTH), F32)
    return pl.pallas_call(
        functools.partial(_delta_scan_kernel, cb=cb, bsz=bsz),
        grid=(nblk,),
        in_specs=specs(0) + specs(1),
        out_specs=[pl.BlockSpec((bsz, cb * C_CHUNK, C_WIDTH), lambda n: (0, n, 0)),
                   pl.BlockSpec((bsz, cb * C_CHUNK, C_WIDTH), lambda n: (0, nblk - 1 - n, 0))],
        out_shape=[out, out],
        scratch_shapes=[pltpu.VMEM((bsz * 2 * C_HEADS, C_HEAD_DIM, C_HEAD_DIM), F32)],
        compiler_params=_cparams(("arbitrary",)),
    )(*arrs, *arrs)


def _merge_kernel(x_ref, oa_ref, ob_ref, ocf_ref, ocb_ref, zc_ref, gates_ref, cg_ref, wa_ref, wb_ref, wc_ref,
                  wo_ref, fg_ref, o_ref, *, final):
    oc = ocf_ref[...] + ocb_ref[...]
    cg = cg_ref[...]
    parts = []
    for h in range(C_HEADS):
        blk = oc[:, h * LANES:(h + 1) * LANES]
        blk = blk * lax.rsqrt(jnp.mean(blk * blk, axis=-1, keepdims=True) + EPS) * cg
        parts.append((blk * zc_ref[:, h * LANES:(h + 1) * LANES].astype(F32)).astype(BF16))
    ocn = jnp.concatenate(parts, axis=1)
    ya = jnp.dot(oa_ref[...], wa_ref[...], preferred_element_type=F32)
    yb = jnp.dot(ob_ref[...], wb_ref[...], preferred_element_type=F32)
    yc = jnp.dot(ocn, wc_ref[...], preferred_element_type=F32)
    d = D_MODEL
    merged = (gates_ref[:, 0:d].astype(F32) * ya + gates_ref[:, d:2 * d].astype(F32) * yb
              + gates_ref[:, 2 * d:3 * d].astype(F32) * yc)
    x = x_ref[...] + jnp.dot(merged.astype(BF16), wo_ref[...], preferred_element_type=F32)
    if final:
        x = x * lax.rsqrt(jnp.mean(x * x, axis=-1, keepdims=True) + EPS) * fg_ref[...]
    o_ref[...] = x


def _merge(x2, oa, ob, ocf, ocb, zc, gates, cg, wa, wb, wc, wo, fg, tm, final):
    rows = x2.shape[0]
    row = lambda i: (i, 0)
    const = lambda i: (0, 0)
    return pl.pallas_call(
        functools.partial(_merge_kernel, final=final),
        grid=(rows // tm,),
        in_specs=[pl.BlockSpec((tm, D_MODEL), row),
                  pl.BlockSpec((tm, A_WIDTH), row), pl.BlockSpec((tm, B_WIDTH), row),
                  pl.BlockSpec((tm, C_WIDTH), row), pl.BlockSpec((tm, C_WIDTH), row),
                  pl.BlockSpec((tm, C_WIDTH), row), pl.BlockSpec((tm, N_BRANCH * D_MODEL), row),
                  pl.BlockSpec((1, LANES), const),
                  pl.BlockSpec((A_WIDTH, D_MODEL), const), pl.BlockSpec((B_WIDTH, D_MODEL), const),
                  pl.BlockSpec((C_WIDTH, D_MODEL), const), pl.BlockSpec((D_MODEL, D_MODEL), const),
                  pl.BlockSpec((1, D_MODEL), const)],
        out_specs=pl.BlockSpec((tm, D_MODEL), row),
        out_shape=jax.ShapeDtypeStruct((rows, D_MODEL), F32),
        compiler_params=_cparams(("parallel",)),
    )(x2, oa, ob, ocf, ocb, zc, gates, cg, wa, wb, wc, wo, fg)


def _rope_tables(seq):
    inv = 1.0 / (ROPE_THETA ** (jnp.arange(0, HEAD_DIM, 2, dtype=F32) / HEAD_DIM))
    ang = jnp.arange(seq, dtype=F32)[:, None] * inv[None, :]
    cos = jnp.tile(jnp.cos(ang), (1, LANES // (HEAD_DIM // 2)))
    sin = jnp.tile(jnp.concatenate([-jnp.sin(ang), jnp.sin(ang)], axis=1), (1, LANES // HEAD_DIM))
    return cos, sin


def kernel(x, norm_g, w_in, a_sink, b_lambda, b_subln_g, c_conv_w, c_a_log, c_dt_bias, c_norm_g,
           w_bo_a, w_bo_b, w_bo_c, w_out, final_g):
    bsz, seq, _ = x.shape
    depth = w_in.shape[0]
    rows = bsz * seq
    tm = min(256, seq)
    tq_a = min(256, seq)
    tq_b = min(512, seq)
    tk_b = min(1024, seq)
    cos, sin = _rope_tables(seq)

    def pair_heads(t, axis):
        shp = t.shape
        t = t.reshape(shp[:axis] + (A_KV_HEADS, A_Q_HEADS // A_KV_HEADS, HEAD_DIM) + shp[axis + 1:])
        return jnp.swapaxes(t, axis, axis + 1).reshape(shp)

    def sec(i):
        return w_in[:, :, _OFF[i]:_OFF[i + 1]]

    qscale = HEAD_DIM ** -0.5
    w_main = jnp.concatenate(
        [pair_heads(sec(0), 2) * qscale, sec(1), sec(2), pair_heads(sec(3), 2), sec(4) * qscale,
         sec(5), sec(6), sec(7), sec(8), sec(9), sec(12)], axis=2).astype(BF16)
    w_ba = w_in[:, :, _OFF[10]:_OFF[12]].astype(BF16)
    dec = jnp.stack([jnp.concatenate([jnp.zeros((depth, 2 * C_HEADS), F32),
                                      -jnp.exp(c_a_log.astype(F32)).reshape(depth, 2 * C_HEADS)], axis=1),
                     jnp.concatenate([jnp.zeros((depth, 2 * C_HEADS), F32),
                                      c_dt_bias.astype(F32).reshape(depth, 2 * C_HEADS)], axis=1)], axis=1)
    wa = pair_heads(w_bo_a, 1).astype(BF16)
    wb = w_bo_b.astype(BF16)
    wc = w_bo_c.astype(BF16)
    wo = w_out.astype(BF16)

    x2 = x.reshape(rows, D_MODEL)
    for l in range(depth):
        (qa, ka, va, za, qb, kb, vb, zb, cqkv, zc, gates, bg) = _inproj(
            x2, norm_g[l][None], w_main[l], w_ba[l], cos, sin, dec[l], seq, tm)

        r3 = lambda t: t.reshape(bsz, seq, t.shape[-1])
        oa = _attn_a(a_sink[l].astype(F32), r3(qa), r3(ka), r3(va), r3(za), tq_a)

        lam_init = 0.8 - 0.6 * math.exp(-0.3 * l)
        bl = b_lambda[l].astype(F32)
        lam = jnp.exp(jnp.sum(bl[0] * bl[1])) - jnp.exp(jnp.sum(bl[2] * bl[3])) + lam_init
        lam_s = jnp.stack([lam, jnp.asarray(1.0 - lam_init, F32)])
        ob = _attn_b(lam_s, r3(qb), r3(kb), r3(vb), r3(zb), b_subln_g[l][None].astype(F32), tq_b, tk_b)

        prep = _delta_prep(cqkv, c_conv_w[l].astype(F32), bg, seq, min(256, seq))
        ocf, ocb = _delta_scan(prep, bsz, seq, 2)

        x2 = _merge(x2, oa.reshape(rows, A_WIDTH), ob.reshape(rows, B_WIDTH),
                    ocf.reshape(rows, C_WIDTH), ocb.reshape(rows, C_WIDTH), zc, gates,
                    c_norm_g[l][None].astype(F32), wa[l], wb[l], wc[l], wo[l],
                    final_g[None].astype(F32), tm, l == depth - 1)
    return x2.reshape(bsz, seq, D_MODEL)
```

```python
import functools
import math

import numpy as np
import jax
import jax.numpy as jnp
from jax import lax
from jax.experimental import pallas as pl
from jax.experimental.pallas import tpu as pltpu

F32 = jnp.float32
BF16 = jnp.bfloat16

D_MODEL = 1024
HEAD_DIM = 64
ROPE_THETA = 10000.0
EPS = 1e-6
A_Q_HEADS = 8
A_KV_HEADS = 2
A_WIDTH = A_Q_HEADS * HEAD_DIM
A_KV_WIDTH = A_KV_HEADS * HEAD_DIM
WINDOW = 128
B_HEADS = 4
B_V_DIM = 2 * HEAD_DIM
B_WIDTH = B_HEADS * B_V_DIM
C_HEADS = 4
C_HEAD_DIM = 128
C_WIDTH = C_HEADS * C_HEAD_DIM
C_CONV = 5
C_CHUNK = 64
N_BRANCH = 3
LOG2E = math.log2(math.e)
LANES = 128
SUBLANES = 8
MXU_N = 256
VMEM_LIMIT = 56 * 1024 * 1024

IN_SPLITS = (A_WIDTH, A_KV_WIDTH, A_KV_WIDTH, A_WIDTH,
             B_WIDTH, B_WIDTH, B_WIDTH, B_WIDTH,
             3 * C_WIDTH, C_WIDTH, 2 * C_HEADS, 2 * C_HEADS,
             N_BRANCH * D_MODEL)
_OFF = [0] + [int(o) for o in np.cumsum(IN_SPLITS)]

_SEC = {}
for _name, _width in (('aq', A_WIDTH), ('ak', A_KV_WIDTH), ('av', A_KV_WIDTH), ('az', A_WIDTH),
                      ('bq', B_WIDTH), ('bk', B_WIDTH), ('bv', B_WIDTH), ('bz', B_WIDTH),
                      ('cqkv', 3 * C_WIDTH)):
    _SEC[_name] = (sum(w for _, w in _SEC.values()), _width)
N_PROJ = sum(w for _, w in _SEC.values())
N_BA = 4 * C_HEADS
N_LATE = C_WIDTH + N_BRANCH * D_MODEL


def _cparams(sem):
    return pltpu.CompilerParams(dimension_semantics=sem, vmem_limit_bytes=VMEM_LIMIT)


def _rope(t, cos, sin_signed, first_half):
    swapped = jnp.where(first_half, pltpu.roll(t, LANES - 32, 1), pltpu.roll(t, 32, 1))
    return t * cos + swapped * sin_signed


def _softplus(t):
    return jnp.maximum(t, 0.0) + jnp.log(1.0 + jnp.exp(-jnp.abs(t)))


def _normed_bf16(x, g):
    return (x * lax.rsqrt(jnp.mean(x * x, axis=-1, keepdims=True) + EPS) * g).astype(BF16)


def _silu(t):
    return t * jax.nn.sigmoid(t)


def _project(hb, w_ref, slabs):
    for pair in range(len(slabs) // 2):
        acc = jnp.dot(hb, w_ref[:, pair * MXU_N:(pair + 1) * MXU_N], preferred_element_type=F32)
        for half in range(2):
            ref, j, fn = slabs[2 * pair + half]
            t = acc[:, half * LANES:(half + 1) * LANES]
            ref[:, j * LANES:(j + 1) * LANES] = (t if fn is None else fn(t)).astype(ref.dtype)


def _inproj_kernel(x_ref, g_ref, w_ref, wba_ref, cos_ref, sin_ref, dec_ref,
                   qa_ref, ka_ref, va_ref, za_ref, qb_ref, kb_ref, vb_ref, zb_ref, cqkv_ref, bg_ref):
    hb = _normed_bf16(x_ref[...], g_ref[...])
    cos = cos_ref[...]
    sin = sin_ref[...]
    first_half = (lax.broadcasted_iota(jnp.int32, cos.shape, 1) & 32) == 0

    def rope(t):
        return _rope(t, cos, sin, first_half)

    def rope_q(t):
        return rope(t) * LOG2E

    epilogues = (('aq', qa_ref, rope_q), ('ak', ka_ref, rope), ('av', va_ref, None), ('az', za_ref, _silu),
                 ('bq', qb_ref, rope_q), ('bk', kb_ref, rope), ('bv', vb_ref, None),
                 ('bz', zb_ref, _silu), ('cqkv', cqkv_ref, None))
    slabs = []
    for name, ref, fn in epilogues:
        assert _SEC[name][0] == len(slabs) * LANES
        slabs += [(ref, j, fn) for j in range(_SEC[name][1] // LANES)]
    _project(hb, w_ref, slabs)

    ba = jnp.dot(hb, wba_ref[...], preferred_element_type=F32)
    is_beta = lax.broadcasted_iota(jnp.int32, ba.shape, 1) < 2 * C_HEADS
    dec = dec_ref[...]
    bg_ref[...] = jnp.where(is_beta, jax.nn.sigmoid(ba), dec[0:1, :] * _softplus(ba + dec[1:2, :]))


def _inproj(x2, g, w, wba, cos, sin, dec, seq, tm):
    rows = x2.shape[0]
    nseq = seq // tm
    row = lambda i: (i, 0)
    const = lambda i: (0, 0)
    single = pl.Buffered(1)
    out_widths = [('aq', BF16), ('ak', BF16), ('av', BF16), ('az', BF16), ('bq', BF16), ('bk', BF16),
                  ('bv', BF16), ('bz', BF16), ('cqkv', F32)]
    out_shape = [jax.ShapeDtypeStruct((rows, _SEC[n][1]), dt) for n, dt in out_widths]
    out_specs = [pl.BlockSpec((tm, _SEC[n][1]), row) for n, _ in out_widths]
    out_shape += [jax.ShapeDtypeStruct((rows, N_BA), F32)]
    out_specs += [pl.BlockSpec((tm, N_BA), row)]
    return pl.pallas_call(
        _inproj_kernel,
        grid=(rows // tm,),
        in_specs=[pl.BlockSpec((tm, D_MODEL), row),
                  pl.BlockSpec((1, D_MODEL), const),
                  pl.BlockSpec((D_MODEL, N_PROJ), const, pipeline_mode=single),
                  pl.BlockSpec((D_MODEL, N_BA), const),
                  pl.BlockSpec((tm, LANES), lambda i: (i % nseq, 0)),
                  pl.BlockSpec((tm, LANES), lambda i: (i % nseq, 0)),
                  pl.BlockSpec((2, N_BA), const)],
        out_specs=out_specs,
        out_shape=out_shape,
        compiler_params=_cparams(("parallel",)),
    )(x2, g, w, wba, cos, sin, dec)


def _attn_a_kernel(sink_ref, q_ref, kp_ref, k_ref, kn_ref, vp_ref, v_ref, vn_ref, z_ref, o_ref, *, tq, seq):
    i = pl.program_id(1)
    kcat = jnp.concatenate([kp_ref[0], k_ref[0], kn_ref[0]], axis=0)
    vcat = jnp.concatenate([vp_ref[0], v_ref[0], vn_ref[0]], axis=0)
    lane = lax.broadcasted_iota(jnp.int32, (1, LANES), 1)
    low = lane < HEAD_DIM
    one = jnp.ones_like(vcat)
    v_lo = jnp.where(low, vcat, one)
    v_hi = jnp.where(low, one, vcat)
    nk = tq + 2 * WINDOW
    r = lax.broadcasted_iota(jnp.int32, (tq, nk), 0)
    c = lax.broadcasted_iota(jnp.int32, (tq, nk), 1)
    kabs = i * tq - WINDOW + c
    valid = (jnp.abs(c - WINDOW - r) <= WINDOW) & (kabs >= 0) & (kabs < seq)

    def probs(qm, sink):
        s = lax.dot_general(qm, kcat, (((1,), (1,)), ((), ())), preferred_element_type=F32)
        s = jnp.where(valid, s, -1e30)
        m = jnp.maximum(jnp.max(s, axis=-1, keepdims=True), sink)
        return jnp.exp2(s - m).astype(BF16), jnp.exp2(sink - m)

    half = A_Q_HEADS // 2
    for b in range(half):
        q = q_ref[0, :, b * LANES:(b + 1) * LANES]
        p0, e0 = probs(jnp.where(low, q, jnp.zeros_like(q)), sink_ref[b] * LOG2E)
        p1, e1 = probs(jnp.where(low, jnp.zeros_like(q), q), sink_ref[half + b] * LOG2E)
        r0 = jnp.dot(p0, v_lo, preferred_element_type=F32)
        r1 = jnp.dot(p1, v_hi, preferred_element_type=F32)
        num = jnp.where(low, r0, r1)
        den = (jnp.where(low, pltpu.roll(r0, HEAD_DIM, 1), pltpu.roll(r1, HEAD_DIM, 1))
               + jnp.where(low, e0, e1))
        o_ref[0, :, b * LANES:(b + 1) * LANES] = (
            num / den * z_ref[0, :, b * LANES:(b + 1) * LANES].astype(F32)).astype(BF16)


def _attn_a(sink, q, k, v, z, tq):
    bsz, seq, _ = q.shape
    r = tq // WINDOW
    nwb = seq // WINDOW
    main = lambda b, i: (b, i, 0)
    prev = lambda b, i: (b, jnp.maximum(i * r - 1, 0), 0)
    nxt = lambda b, i: (b, jnp.minimum((i + 1) * r, nwb - 1), 0)
    kv_specs = [pl.BlockSpec((1, WINDOW, LANES), prev), pl.BlockSpec((1, tq, LANES), main),
                pl.BlockSpec((1, WINDOW, LANES), nxt)]
    return pl.pallas_call(
        functools.partial(_attn_a_kernel, tq=tq, seq=seq),
        grid=(bsz, seq // tq),
        in_specs=[pl.BlockSpec(memory_space=pltpu.SMEM), pl.BlockSpec((1, tq, A_WIDTH), main)]
                 + kv_specs + kv_specs + [pl.BlockSpec((1, tq, A_WIDTH), main)],
        out_specs=pl.BlockSpec((1, tq, A_WIDTH), main),
        out_shape=jax.ShapeDtypeStruct((bsz, seq, A_WIDTH), BF16),
        compiler_params=_cparams(("parallel", "parallel")),
    )(sink, q, k, k, k, v, v, v, z)


def _attn_b_kernel(lam_ref, q_ref, k_ref, v_ref, z_ref, g_ref, o_ref, acc_ref, m_ref, s_ref,
                   *, tq, tk, seq):
    q = q_ref[0]
    low = lax.broadcasted_iota(jnp.int32, (1, LANES), 1) < HEAD_DIM
    zero = jnp.zeros_like(q)
    q2 = jnp.concatenate([jnp.where(low, q, zero), jnp.where(low, zero, q)], axis=0)
    acc_ref[...] = jnp.zeros_like(acc_ref)
    m_ref[...] = jnp.full_like(m_ref, -jnp.inf)
    ones = jnp.ones((MXU_N, LANES), BF16)
    nk = seq // tk

    def scores(j):
        return lax.dot_general(q2, k_ref[0, j * tk:(j + 1) * tk, :], (((1,), (1,)), ((), ())),
                               preferred_element_type=F32)

    def update(s_ref, j):
        m_old = m_ref[...]
        m_new = jnp.maximum(m_old, jnp.max(s_ref[...], axis=-1, keepdims=True))
        m_ref[...] = m_new
        pv = None
        for kb in range(tk // MXU_N):
            lo = kb * MXU_N
            p = jnp.concatenate(
                [jnp.exp2(s_ref[:, lo + c * LANES:lo + (c + 1) * LANES] - m_ref[...]).astype(BF16)
                 for c in range(MXU_N // LANES)], axis=1)
            vc = jnp.concatenate([v_ref[0, j * tk + lo:j * tk + lo + MXU_N, :], ones], axis=1)
            d = jnp.dot(p, vc, preferred_element_type=F32)
            pv = d if pv is None else pv + d
        alpha = jnp.exp2(m_old - m_new)
        acc_ref[...] = acc_ref[...] * jnp.concatenate([alpha, alpha], axis=1) + pv

    nbuf = s_ref.shape[0]
    s_ref[0] = scores(0)
    for j in range(nk):
        if j + 1 < nk:
            s_ref[(j + 1) % nbuf] = scores(j + 1)
        update(s_ref.at[j % nbuf], j)
    acc = acc_ref[...]
    o1 = acc[:tq, :LANES] / acc[:tq, LANES:]
    o2 = acc[tq:, :LANES] / acc[tq:, LANES:]
    o = o1 - lam_ref[0] * o2
    o = o * lax.rsqrt(jnp.mean(o * o, axis=-1, keepdims=True) + EPS) * g_ref[...] * lam_ref[1]
    o_ref[0] = (o * z_ref[0].astype(F32)).astype(BF16)


def _attn_b(lam, q, k, v, z, g, tq, tk):
    bsz, seq, _ = q.shape
    assert seq % tk == 0 and seq % tq == 0
    qmap = lambda b, h, i: (b, i, h)
    kmap = lambda b, h, i: (b, 0, h)
    return pl.pallas_call(
        functools.partial(_attn_b_kernel, tq=tq, tk=tk, seq=seq),
        grid=(bsz, B_HEADS, seq // tq),
        in_specs=[pl.BlockSpec(memory_space=pltpu.SMEM),
                  pl.BlockSpec((1, tq, LANES), qmap),
                  pl.BlockSpec((1, seq, LANES), kmap),
                  pl.BlockSpec((1, seq, LANES), kmap),
                  pl.BlockSpec((1, tq, LANES), qmap),
                  pl.BlockSpec((1, LANES), lambda b, h, i: (0, 0))],
        out_specs=pl.BlockSpec((1, tq, LANES), qmap),
        out_shape=jax.ShapeDtypeStruct((bsz, seq, B_WIDTH), BF16),
        scratch_shapes=[pltpu.VMEM((2 * tq, 2 * LANES), F32), pltpu.VMEM((2 * tq, LANES), F32),
                        pltpu.VMEM((2, 2 * tq, tk), F32)],
        compiler_params=_cparams(("parallel", "parallel", "parallel")),
    )(lam, q, k, v, z, g)


def _short_conv(xp_ref, x_ref, xn_ref, w_ref, buf_ref, qkv_ref, *, tc, seq):
    pos = (pl.program_id(0) * tc) % seq
    halo = (C_CONV - 1) // 2
    buf_ref[0:SUBLANES, :] = jnp.where(pos == 0, 0.0, xp_ref[...])
    buf_ref[SUBLANES:SUBLANES + tc, :] = x_ref[...]
    buf_ref[SUBLANES + tc:, :] = jnp.where(pos + tc == seq, 0.0, xn_ref[...])
    for j in range(3 * C_HEADS):
        cols = slice(j * LANES, (j + 1) * LANES)
        y = jnp.zeros((tc, LANES), F32)
        for t in range(C_CONV):
            y = y + buf_ref[SUBLANES - halo + t:SUBLANES - halo + t + tc, cols] * w_ref[t:t + 1, cols]
        y = y * jax.nn.sigmoid(y)
        if j < 2 * C_HEADS:
            y = y * lax.rsqrt(jnp.sum(y * y, axis=-1, keepdims=True) + EPS)
        if j < C_HEADS:
            y = y * (C_HEAD_DIM ** -0.5)
        qkv_ref[:, cols] = y


HC = C_HEADS * C_CHUNK
CHUNK_SHIFT = 6


def _mm(a, b):
    return jnp.dot(a, b, preferred_element_type=F32)


def _split3(x):
    hi = x.astype(BF16)
    r = x - hi.astype(F32)
    mid = r.astype(BF16)
    return hi, mid, (r - mid.astype(F32)).astype(BF16)


def _block_diag(x, mask):
    return jnp.where(mask, jnp.concatenate([x] * C_HEADS, axis=0), jnp.zeros((), x.dtype))


def _delta_prep_kernel(xp_ref, x_ref, xn_ref, cw_ref, bg_ref, w_ref, u_ref, qd_ref, ktt_ref, att_ref, eg_ref,
                       buf_ref, qkv_ref, *, rb, seq):
    c = C_CHUNK
    _short_conv(xp_ref, x_ref, xn_ref, cw_ref, buf_ref, qkv_ref, tc=rb, seq=seq)
    bg = bg_ref[...]
    g3 = _split3(bg)
    rr = lax.broadcasted_iota(jnp.int32, (rb, rb), 0)
    cc = lax.broadcasted_iota(jnp.int32, (rb, rb), 1)
    same = (rr >> CHUNK_SHIFT) == (cc >> CHUNK_SHIFT)
    tri = (jnp.where(same & (rr >= cc), 1.0, 0.0).astype(BF16),
           jnp.where(same & (rr <= cc), 1.0, 0.0).astype(BF16))
    gcum = [_mm(t, g3[0]) + _mm(t, g3[1]) + _mm(t, g3[2]) for t in tri]

    row = lax.broadcasted_iota(jnp.int32, (c, HC), 0)
    lm = lax.broadcasted_iota(jnp.int32, (c, HC), 1) & (c - 1)
    eye_cat = row == lm
    lane_lo = lax.broadcasted_iota(jnp.int32, (c, LANES), 1) < c
    bdmask = ((lax.broadcasted_iota(jnp.int32, (HC, HC), 0) >> CHUNK_SHIFT)
              == (lax.broadcasted_iota(jnp.int32, (HC, HC), 1) >> CHUNK_SHIFT))
    zeros_blk = jnp.zeros((c, LANES), BF16)
    heads = range(C_HEADS)

    def col(a, j):
        return jnp.broadcast_to(a[:, j:j + 1], (c, LANES))

    def spread(cols):
        return jnp.concatenate([jnp.where(lane_lo, cols[0], cols[1]),
                                jnp.where(lane_lo, cols[2], cols[3])], axis=1)

    amats = []
    rhss = []
    for ci in range(rb // c):
        rows = slice(ci * c, (ci + 1) * c)
        q = [qkv_ref[rows, h * LANES:(h + 1) * LANES] for h in heads]
        k = [qkv_ref[rows, C_WIDTH + h * LANES:C_WIDTH + (h + 1) * LANES] for h in heads]
        v = [qkv_ref[rows, 2 * C_WIDTH + h * LANES:2 * C_WIDTH + (h + 1) * LANES] for h in heads]
        kb = [t.astype(BF16) for t in k]
        lhs = jnp.concatenate([jnp.concatenate(kb, axis=1),
                               jnp.concatenate([t.astype(BF16) for t in q], axis=1)], axis=0)
        kmask = jnp.concatenate(
            [jnp.concatenate([kb[h] if j == h else zeros_blk for j in heads], axis=1) for h in heads],
            axis=0)
        gram = lax.dot_general(lhs, kmask, (((1,), (1,)), ((), ())), preferred_element_type=F32)
        kk = gram[:c]
        qk = gram[c:]
        beta_all = bg[rows]
        for d in range(2):
            gc = gcum[d][rows]
            off_b = d * C_HEADS
            off_g = 2 * C_HEADS + d * C_HEADS
            bcol = [col(beta_all, off_b + h) for h in heads]
            gcol = [col(gc, off_g + h) for h in heads]
            g_last = gc[c - 1:c, :] if d == 0 else gc[0:1, :]
            glcol = [jnp.broadcast_to(g_last[:, off_g + h:off_g + h + 1], (c, LANES)) for h in heads]
            gc_sp = spread(gcol)
            gc_row = jnp.sum(jnp.where(eye_cat, gc_sp, 0.0), axis=0, keepdims=True)
            incl = (row >= lm) if d == 0 else (row <= lm)
            strict = (row > lm) if d == 0 else (row < lm)
            decay = jnp.where(incl, jnp.exp(jnp.where(incl, gc_sp - gc_row, 0.0)), 0.0)
            amats.append(jnp.where(strict, spread(bcol) * kk * decay, 0.0))
            egc = [jnp.exp(gcol[h]) for h in heads]
            rhss.append(jnp.concatenate(
                [jnp.concatenate([v[h] * bcol[h], k[h] * (bcol[h] * egc[h])], axis=1) for h in heads],
                axis=0))
            qd_ref[ci, d] = jnp.concatenate([q[h] * egc[h] for h in heads], axis=0).astype(BF16)
            kt = jnp.concatenate([k[h] * jnp.exp(glcol[h] - gcol[h]) for h in heads], axis=0)
            ktt_ref[ci, d] = kt.T.astype(BF16)
            att_ref[ci, d] = jnp.where(incl, qk * decay, 0.0).astype(BF16)
            eg_ref[ci, d] = jnp.exp(jnp.concatenate([glcol[h][0:1, :] for h in heads], axis=0))

    units = range(len(amats))
    tms = [-a for a in amats]
    abs_ = [a.astype(BF16) for a in amats]
    pws = [_mm(abs_[i], _block_diag(abs_[i], bdmask)) for i in units]
    for _ in range(CHUNK_SHIFT - 2):
        pbs = [p.astype(BF16) for p in pws]
        both = [_mm(jnp.concatenate([tms[i].astype(BF16), pbs[i]], axis=0), _block_diag(pbs[i], bdmask))
                for i in units]
        tms = [tms[i] + pws[i] + both[i][:c] for i in units]
        pws = [both[i][c:] for i in units]
    last = [_mm(tms[i].astype(BF16), _block_diag(pws[i].astype(BF16), bdmask)) for i in units]
    tms = [tms[i] + pws[i] + last[i] for i in units]
    uws = [_mm(_block_diag(tms[i].astype(BF16), bdmask), rhss[i].astype(BF16)) for i in units]
    for i in units:
        uw = rhss[i] + uws[i]
        u_ref[i // 2, i % 2] = uw[:, :LANES]
        w_ref[i // 2, i % 2] = uw[:, LANES:].astype(BF16)


def _delta_prep(x2, conv_w, bg2, seq, rb):
    rows, width = x2.shape
    nct = rows // C_CHUNK
    nc = rb // C_CHUNK
    r = rb // SUBLANES
    nb8 = rows // SUBLANES
    blk = lambda i: (i, 0, 0, 0)
    shapes = [((HC, LANES), BF16), ((HC, LANES), F32), ((HC, LANES), BF16),
              ((C_HEAD_DIM, HC), BF16), ((C_CHUNK, HC), BF16), ((C_HEADS, LANES), F32)]
    return pl.pallas_call(
        functools.partial(_delta_prep_kernel, rb=rb, seq=seq),
        grid=(rows // rb,),
        in_specs=[pl.BlockSpec((SUBLANES, width), lambda i: (jnp.maximum(i * r - 1, 0), 0)),
                  pl.BlockSpec((rb, width), lambda i: (i, 0)),
                  pl.BlockSpec((SUBLANES, width), lambda i: (jnp.minimum((i + 1) * r, nb8 - 1), 0)),
                  pl.BlockSpec((C_CONV, width), lambda i: (0, 0)),
                  pl.BlockSpec((rb, N_BA), lambda i: (i, 0))],
        out_specs=[pl.BlockSpec((nc, 2) + s, blk) for s, _ in shapes],
        out_shape=[jax.ShapeDtypeStruct((nct, 2) + s, dt) for s, dt in shapes],
        scratch_shapes=[pltpu.VMEM((rb + 2 * SUBLANES, width), F32), pltpu.VMEM((rb, width), F32)],
        compiler_params=_cparams(("parallel",)),
    )(x2, x2, x2, conv_w, bg2)


def _delta_scan_kernel(*refs, cb, bsz):
    ins = (refs[0:6], refs[6:12])
    outs = refs[12:14]
    state_ref = refs[14]
    c = C_CHUNK

    @pl.when(pl.program_id(0) == 0)
    def _():
        state_ref[...] = jnp.zeros_like(state_ref)

    bdmask = ((lax.broadcasted_iota(jnp.int32, (HC, HC), 0) >> CHUNK_SHIFT)
              == (lax.broadcasted_iota(jnp.int32, (HC, HC), 1) >> CHUNK_SHIFT))
    lane_head = lax.broadcasted_iota(jnp.int32, (1, HC), 1) >> CHUNK_SHIFT
    heads = range(C_HEADS)
    chains = [(b, d) for b in range(bsz) for d in range(2)]
    for step in range(cb):
        ci = {d: step if d == 0 else cb - 1 - step for d in range(2)}
        state, prod = {}, {}
        for b, d in chains:
            w = ins[d][0][b, ci[d]]
            qd = ins[d][2][b, ci[d]]
            base = (b * 2 + d) * C_HEADS
            state[b, d] = [state_ref[base + h] for h in heads]
            prod[b, d] = [_mm(jnp.concatenate([w[h * c:(h + 1) * c], qd[h * c:(h + 1) * c]], axis=0),
                              state[b, d][h].astype(BF16)) for h in heads]
        vnb, att_v = {}, {}
        for b, d in chains:
            ws = jnp.concatenate([p[:c] for p in prod[b, d]], axis=0)
            vnb[b, d] = (ins[d][1][b, ci[d]] - ws).astype(BF16)
            att_v[b, d] = _mm(_block_diag(ins[d][4][b, ci[d]], bdmask), vnb[b, d])
        for b, d in chains:
            ktt = ins[d][3][b, ci[d]]
            eg = ins[d][5][b, ci[d]]
            base = (b * 2 + d) * C_HEADS
            for h in heads:
                kth = jnp.where(lane_head == h, ktt, jnp.zeros((), ktt.dtype))
                state_ref[base + h] = state[b, d][h] * eg[h:h + 1, :] + _mm(kth, vnb[b, d])
                outs[d][b, ci[d] * c:(ci[d] + 1) * c, h * LANES:(h + 1) * LANES] = (
                    prod[b, d][h][c:] + att_v[b, d][h * c:(h + 1) * c])


def _delta_scan(prep, bsz, seq, cb):
    nch = seq // C_CHUNK
    nblk = nch // cb
    arrs = [a.reshape((bsz, nch) + a.shape[1:]) for a in prep]

    def specs(d):
        idx = (lambda n: (0, n, d, 0, 0)) if d == 0 else (lambda n: (0, nblk - 1 - n, d, 0, 0))
        return [pl.BlockSpec((bsz, cb, None) + a.shape[3:], idx) for a in arrs]

    out = jax.ShapeDtypeStruct((bsz, seq, C_WIDTH), F32)
    return pl.pallas_call(
        functools.partial(_delta_scan_kernel, cb=cb, bsz=bsz),
        grid=(nblk,),
        in_specs=specs(0) + specs(1),
        out_specs=[pl.BlockSpec((bsz, cb * C_CHUNK, C_WIDTH), lambda n: (0, n, 0)),
                   pl.BlockSpec((bsz, cb * C_CHUNK, C_WIDTH), lambda n: (0, nblk - 1 - n, 0))],
        out_shape=[out, out],
        scratch_shapes=[pltpu.VMEM((bsz * 2 * C_HEADS, C_HEAD_DIM, C_HEAD_DIM), F32)],
        compiler_params=_cparams(("arbitrary",)),
    )(*arrs, *arrs)


def _merge_kernel(x_ref, g_ref, wl_ref, oa_ref, ob_ref, ocf_ref, ocb_ref, cg_ref, wa_ref, wb_ref, wc_ref,
                  wo_ref, fg_ref, o_ref, zc_ref, gates_ref, *, final):
    hb = _normed_bf16(x_ref[...], g_ref[...])
    slabs = ([(zc_ref, j, _silu) for j in range(C_WIDTH // LANES)]
             + [(gates_ref, j, jax.nn.sigmoid) for j in range(N_BRANCH * D_MODEL // LANES)])
    _project(hb, wl_ref, slabs)
    oc = ocf_ref[...] + ocb_ref[...]
    cg = cg_ref[...]
    parts = []
    for h in range(C_HEADS):
        blk = oc[:, h * LANES:(h + 1) * LANES]
        blk = blk * lax.rsqrt(jnp.mean(blk * blk, axis=-1, keepdims=True) + EPS) * cg
        parts.append((blk * zc_ref[:, h * LANES:(h + 1) * LANES].astype(F32)).astype(BF16))
    ocn = jnp.concatenate(parts, axis=1)
    ya = jnp.dot(oa_ref[...], wa_ref[...], preferred_element_type=F32)
    yb = jnp.dot(ob_ref[...], wb_ref[...], preferred_element_type=F32)
    yc = jnp.dot(ocn, wc_ref[...], preferred_element_type=F32)
    d = D_MODEL
    merged = (gates_ref[:, 0:d].astype(F32) * ya + gates_ref[:, d:2 * d].astype(F32) * yb
              + gates_ref[:, 2 * d:3 * d].astype(F32) * yc)
    x = x_ref[...] + jnp.dot(merged.astype(BF16), wo_ref[...], preferred_element_type=F32)
    if final:
        x = x * lax.rsqrt(jnp.mean(x * x, axis=-1, keepdims=True) + EPS) * fg_ref[...]
    o_ref[...] = x


def _merge(x2, g, wl, oa, ob, ocf, ocb, cg, wa, wb, wc, wo, fg, tm, final):
    rows = x2.shape[0]
    row = lambda i: (i, 0)
    const = lambda i: (0, 0)
    return pl.pallas_call(
        functools.partial(_merge_kernel, final=final),
        grid=(rows // tm,),
        in_specs=[pl.BlockSpec((tm, D_MODEL), row),
                  pl.BlockSpec((1, D_MODEL), const),
                  pl.BlockSpec((D_MODEL, N_LATE), const),
                  pl.BlockSpec((tm, A_WIDTH), row), pl.BlockSpec((tm, B_WIDTH), row),
                  pl.BlockSpec((tm, C_WIDTH), row), pl.BlockSpec((tm, C_WIDTH), row),
                  pl.BlockSpec((1, LANES), const),
                  pl.BlockSpec((A_WIDTH, D_MODEL), const), pl.BlockSpec((B_WIDTH, D_MODEL), const),
                  pl.BlockSpec((C_WIDTH, D_MODEL), const), pl.BlockSpec((D_MODEL, D_MODEL), const),
                  pl.BlockSpec((1, D_MODEL), const)],
        out_specs=pl.BlockSpec((tm, D_MODEL), row),
        out_shape=jax.ShapeDtypeStruct((rows, D_MODEL), F32),
        scratch_shapes=[pltpu.VMEM((tm, C_WIDTH), BF16), pltpu.VMEM((tm, N_BRANCH * D_MODEL), BF16)],
        compiler_params=_cparams(("parallel",)),
    )(x2, g, wl, oa, ob, ocf, ocb, cg, wa, wb, wc, wo, fg)


def _rope_tables(seq):
    inv = 1.0 / (ROPE_THETA ** (jnp.arange(0, HEAD_DIM, 2, dtype=F32) / HEAD_DIM))
    ang = jnp.arange(seq, dtype=F32)[:, None] * inv[None, :]
    cos = jnp.tile(jnp.cos(ang), (1, LANES // (HEAD_DIM // 2)))
    sin = jnp.tile(jnp.concatenate([-jnp.sin(ang), jnp.sin(ang)], axis=1), (1, LANES // HEAD_DIM))
    return cos, sin


def kernel(x, norm_g, w_in, a_sink, b_lambda, b_subln_g, c_conv_w, c_a_log, c_dt_bias, c_norm_g,
           w_bo_a, w_bo_b, w_bo_c, w_out, final_g):
    bsz, seq, _ = x.shape
    depth = w_in.shape[0]
    rows = bsz * seq
    tm = min(256, seq)
    tq_a = min(256, seq)
    tq_b = min(512, seq)
    tk_b = min(1024, seq)
    cos, sin = _rope_tables(seq)

    def pair_heads(t, axis):
        shp = t.shape
        t = t.reshape(shp[:axis] + (A_KV_HEADS, A_Q_HEADS // A_KV_HEADS, HEAD_DIM) + shp[axis + 1:])
        return jnp.swapaxes(t, axis, axis + 1).reshape(shp)

    def sec(i, scale=None):
        t = w_in[:, :, _OFF[i]:_OFF[i + 1]]
        return (t if scale is None else t * scale).astype(BF16)

    qscale = HEAD_DIM ** -0.5
    w_main = jnp.concatenate(
        [pair_heads(sec(0, qscale), 2), sec(1), sec(2), pair_heads(sec(3), 2), sec(4, qscale),
         sec(5), sec(6), sec(7), sec(8)], axis=2)
    w_late = jnp.concatenate([sec(9), sec(12)], axis=2)
    w_ba = w_in[:, :, _OFF[10]:_OFF[12]].astype(BF16)
    dec = jnp.stack([jnp.concatenate([jnp.zeros((depth, 2 * C_HEADS), F32),
                                      -jnp.exp(c_a_log.astype(F32)).reshape(depth, 2 * C_HEADS)], axis=1),
                     jnp.concatenate([jnp.zeros((depth, 2 * C_HEADS), F32),
                                      c_dt_bias.astype(F32).reshape(depth, 2 * C_HEADS)], axis=1)], axis=1)
    wa = pair_heads(w_bo_a, 1).astype(BF16)
    wb = w_bo_b.astype(BF16)
    wc = w_bo_c.astype(BF16)
    wo = w_out.astype(BF16)

    x2 = x.reshape(rows, D_MODEL)
    for l in range(depth):
        (qa, ka, va, za, qb, kb, vb, zb, cqkv, bg) = _inproj(
            x2, norm_g[l][None], w_main[l], w_ba[l], cos, sin, dec[l], seq, tm)

        r3 = lambda t: t.reshape(bsz, seq, t.shape[-1])
        oa = _attn_a(a_sink[l].astype(F32), r3(qa), r3(ka), r3(va), r3(za), tq_a)

        lam_init = 0.8 - 0.6 * math.exp(-0.3 * l)
        bl = b_lambda[l].astype(F32)
        lam = jnp.exp(jnp.sum(bl[0] * bl[1])) - jnp.exp(jnp.sum(bl[2] * bl[3])) + lam_init
        lam_s = jnp.stack([lam, jnp.asarray(1.0 - lam_init, F32)])
        ob = _attn_b(lam_s, r3(qb), r3(kb), r3(vb), r3(zb), b_subln_g[l][None].astype(F32), tq_b, tk_b)

        prep = _delta_prep(cqkv, c_conv_w[l].astype(F32), bg, seq, min(256, seq))
        ocf, ocb = _delta_scan(prep, bsz, seq, 2)

        x2 = _merge(x2, norm_g[l][None], w_late[l], oa.reshape(rows, A_WIDTH), ob.reshape(rows, B_WIDTH),
                    ocf.reshape(rows, C_WIDTH), ocb.reshape(rows, C_WIDTH),
                    c_norm_g[l][None].astype(F32), wa[l], wb[l], wc[l], wo[l],
                    final_g[None].astype(F32), tm, l == depth - 1)
    return x2.reshape(bsz, seq, D_MODEL)
```

```python
import functools
import math

import numpy as np
import jax
import jax.numpy as jnp
from jax import lax
from jax.experimental import pallas as pl
from jax.experimental.pallas import tpu as pltpu

F32 = jnp.float32
BF16 = jnp.bfloat16

D_MODEL = 1024
HEAD_DIM = 64
ROPE_THETA = 10000.0
EPS = 1e-6
A_Q_HEADS = 8
A_KV_HEADS = 2
A_WIDTH = A_Q_HEADS * HEAD_DIM
A_KV_WIDTH = A_KV_HEADS * HEAD_DIM
WINDOW = 128
B_HEADS = 4
B_V_DIM = 2 * HEAD_DIM
B_WIDTH = B_HEADS * B_V_DIM
C_HEADS = 4
C_HEAD_DIM = 128
C_WIDTH = C_HEADS * C_HEAD_DIM
C_CONV = 5
C_CHUNK = 64
N_BRANCH = 3
LOG2E = math.log2(math.e)
LANES = 128
SUBLANES = 8
MXU_N = 256
VMEM_LIMIT = 56 * 1024 * 1024

IN_SPLITS = (A_WIDTH, A_KV_WIDTH, A_KV_WIDTH, A_WIDTH,
             B_WIDTH, B_WIDTH, B_WIDTH, B_WIDTH,
             3 * C_WIDTH, C_WIDTH, 2 * C_HEADS, 2 * C_HEADS,
             N_BRANCH * D_MODEL)
_OFF = [0] + [int(o) for o in np.cumsum(IN_SPLITS)]

_SEC = {}
for _name, _width in (('aq', A_WIDTH), ('ak', A_KV_WIDTH), ('av', A_KV_WIDTH), ('az', A_WIDTH),
                      ('bq', B_WIDTH), ('bk', B_WIDTH), ('bv', B_WIDTH), ('bz', B_WIDTH),
                      ('cqkv', 3 * C_WIDTH)):
    _SEC[_name] = (sum(w for _, w in _SEC.values()), _width)
N_PROJ = sum(w for _, w in _SEC.values())
N_BA = 4 * C_HEADS
N_LATE = C_WIDTH + N_BRANCH * D_MODEL


def _cparams(sem):
    return pltpu.CompilerParams(dimension_semantics=sem, vmem_limit_bytes=VMEM_LIMIT)


def _rope(t, cos, sin_signed, first_half):
    swapped = jnp.where(first_half, pltpu.roll(t, LANES - 32, 1), pltpu.roll(t, 32, 1))
    return t * cos + swapped * sin_signed


def _softplus(t):
    return jnp.maximum(t, 0.0) + jnp.log(1.0 + jnp.exp(-jnp.abs(t)))


def _normed_bf16(x, g):
    return (x * lax.rsqrt(jnp.mean(x * x, axis=-1, keepdims=True) + EPS) * g).astype(BF16)


def _silu(t):
    return t * jax.nn.sigmoid(t)


def _project(hb, w_ref, slabs):
    for pair in range(len(slabs) // 2):
        acc = jnp.dot(hb, w_ref[:, pair * MXU_N:(pair + 1) * MXU_N], preferred_element_type=F32)
        for half in range(2):
            ref, j, fn = slabs[2 * pair + half]
            t = acc[:, half * LANES:(half + 1) * LANES]
            ref[:, j * LANES:(j + 1) * LANES] = (t if fn is None else fn(t)).astype(ref.dtype)


def _inproj_kernel(x_ref, g_ref, w_ref, wba_ref, cos_ref, sin_ref, dec_ref,
                   qa_ref, ka_ref, va_ref, za_ref, qb_ref, kb_ref, vb_ref, zb_ref, cqkv_ref, bg_ref):
    hb = _normed_bf16(x_ref[...], g_ref[...])
    cos = cos_ref[...]
    sin = sin_ref[...]
    first_half = (lax.broadcasted_iota(jnp.int32, cos.shape, 1) & 32) == 0

    def rope(t):
        return _rope(t, cos, sin, first_half)

    def rope_q(t):
        return rope(t) * LOG2E

    epilogues = (('aq', qa_ref, rope_q), ('ak', ka_ref, rope), ('av', va_ref, None), ('az', za_ref, _silu),
                 ('bq', qb_ref, rope_q), ('bk', kb_ref, rope), ('bv', vb_ref, None),
                 ('bz', zb_ref, _silu), ('cqkv', cqkv_ref, None))
    slabs = []
    for name, ref, fn in epilogues:
        assert _SEC[name][0] == len(slabs) * LANES
        slabs += [(ref, j, fn) for j in range(_SEC[name][1] // LANES)]
    _project(hb, w_ref, slabs)

    ba = jnp.dot(hb, wba_ref[...], preferred_element_type=F32)
    is_beta = lax.broadcasted_iota(jnp.int32, ba.shape, 1) < 2 * C_HEADS
    dec = dec_ref[...]
    bg_ref[...] = jnp.where(is_beta, jax.nn.sigmoid(ba), dec[0:1, :] * _softplus(ba + dec[1:2, :]))


def _inproj(x2, g, w, wba, cos, sin, dec, seq, tm):
    rows = x2.shape[0]
    nseq = seq // tm
    row = lambda i: (i, 0)
    const = lambda i: (0, 0)
    single = pl.Buffered(1)
    out_widths = [('aq', BF16), ('ak', BF16), ('av', BF16), ('az', BF16), ('bq', BF16), ('bk', BF16),
                  ('bv', BF16), ('bz', BF16), ('cqkv', F32)]
    out_shape = [jax.ShapeDtypeStruct((rows, _SEC[n][1]), dt) for n, dt in out_widths]
    out_specs = [pl.BlockSpec((tm, _SEC[n][1]), row) for n, _ in out_widths]
    out_shape += [jax.ShapeDtypeStruct((rows, N_BA), F32)]
    out_specs += [pl.BlockSpec((tm, N_BA), row)]
    return pl.pallas_call(
        _inproj_kernel,
        grid=(rows // tm,),
        in_specs=[pl.BlockSpec((tm, D_MODEL), row),
                  pl.BlockSpec((1, D_MODEL), const),
                  pl.BlockSpec((D_MODEL, N_PROJ), const, pipeline_mode=single),
                  pl.BlockSpec((D_MODEL, N_BA), const),
                  pl.BlockSpec((tm, LANES), lambda i: (i % nseq, 0)),
                  pl.BlockSpec((tm, LANES), lambda i: (i % nseq, 0)),
                  pl.BlockSpec((2, N_BA), const)],
        out_specs=out_specs,
        out_shape=out_shape,
        compiler_params=_cparams(("parallel",)),
    )(x2, g, w, wba, cos, sin, dec)


def _attn_a_kernel(sink_ref, q_ref, kp_ref, k_ref, kn_ref, vp_ref, v_ref, vn_ref, z_ref, o_ref, *, tq, seq):
    i = pl.program_id(1)
    kcat = jnp.concatenate([kp_ref[0], k_ref[0], kn_ref[0]], axis=0)
    vcat = jnp.concatenate([vp_ref[0], v_ref[0], vn_ref[0]], axis=0)
    lane = lax.broadcasted_iota(jnp.int32, (1, LANES), 1)
    low = lane < HEAD_DIM
    one = jnp.ones_like(vcat)
    v_lo = jnp.where(low, vcat, one)
    v_hi = jnp.where(low, one, vcat)
    nk = tq + 2 * WINDOW
    r = lax.broadcasted_iota(jnp.int32, (tq, nk), 0)
    c = lax.broadcasted_iota(jnp.int32, (tq, nk), 1)
    kabs = i * tq - WINDOW + c
    valid = (jnp.abs(c - WINDOW - r) <= WINDOW) & (kabs >= 0) & (kabs < seq)

    def probs(qm, sink):
        s = lax.dot_general(qm, kcat, (((1,), (1,)), ((), ())), preferred_element_type=F32)
        s = jnp.where(valid, s, -1e30)
        m = jnp.maximum(jnp.max(s, axis=-1, keepdims=True), sink)
        return jnp.exp2(s - m).astype(BF16), jnp.exp2(sink - m)

    half = A_Q_HEADS // 2
    for b in range(half):
        q = q_ref[0, :, b * LANES:(b + 1) * LANES]
        p0, e0 = probs(jnp.where(low, q, jnp.zeros_like(q)), sink_ref[b] * LOG2E)
        p1, e1 = probs(jnp.where(low, jnp.zeros_like(q), q), sink_ref[half + b] * LOG2E)
        r0 = jnp.dot(p0, v_lo, preferred_element_type=F32)
        r1 = jnp.dot(p1, v_hi, preferred_element_type=F32)
        num = jnp.where(low, r0, r1)
        den = (jnp.where(low, pltpu.roll(r0, HEAD_DIM, 1), pltpu.roll(r1, HEAD_DIM, 1))
               + jnp.where(low, e0, e1))
        o_ref[0, :, b * LANES:(b + 1) * LANES] = (
            num / den * z_ref[0, :, b * LANES:(b + 1) * LANES].astype(F32)).astype(BF16)


def _attn_a(sink, q, k, v, z, tq):
    bsz, seq, _ = q.shape
    r = tq // WINDOW
    nwb = seq // WINDOW
    main = lambda b, i: (b, i, 0)
    prev = lambda b, i: (b, jnp.maximum(i * r - 1, 0), 0)
    nxt = lambda b, i: (b, jnp.minimum((i + 1) * r, nwb - 1), 0)
    kv_specs = [pl.BlockSpec((1, WINDOW, LANES), prev), pl.BlockSpec((1, tq, LANES), main),
                pl.BlockSpec((1, WINDOW, LANES), nxt)]
    return pl.pallas_call(
        functools.partial(_attn_a_kernel, tq=tq, seq=seq),
        grid=(bsz, seq // tq),
        in_specs=[pl.BlockSpec(memory_space=pltpu.SMEM), pl.BlockSpec((1, tq, A_WIDTH), main)]
                 + kv_specs + kv_specs + [pl.BlockSpec((1, tq, A_WIDTH), main)],
        out_specs=pl.BlockSpec((1, tq, A_WIDTH), main),
        out_shape=jax.ShapeDtypeStruct((bsz, seq, A_WIDTH), BF16),
        compiler_params=_cparams(("parallel", "parallel")),
    )(sink, q, k, k, k, v, v, v, z)


def _attn_b_kernel(lam_ref, q_ref, k_ref, v_ref, z_ref, g_ref, o_ref, acc_ref, m_ref, s_ref,
                   *, tq, tk, seq):
    q = q_ref[0]
    low = lax.broadcasted_iota(jnp.int32, (1, LANES), 1) < HEAD_DIM
    zero = jnp.zeros_like(q)
    q2 = jnp.concatenate([jnp.where(low, q, zero), jnp.where(low, zero, q)], axis=0)
    acc_ref[...] = jnp.zeros_like(acc_ref)
    m_ref[...] = jnp.full_like(m_ref, -jnp.inf)
    ones = jnp.ones((MXU_N, LANES), BF16)
    nk = seq // tk

    def scores(j):
        return lax.dot_general(q2, k_ref[0, j * tk:(j + 1) * tk, :], (((1,), (1,)), ((), ())),
                               preferred_element_type=F32)

    def update(s_ref, j):
        m_old = m_ref[...]
        m_new = jnp.maximum(m_old, jnp.max(s_ref[...], axis=-1, keepdims=True))
        m_ref[...] = m_new
        pv = None
        for kb in range(tk // MXU_N):
            lo = kb * MXU_N
            p = jnp.concatenate(
                [jnp.exp2(s_ref[:, lo + c * LANES:lo + (c + 1) * LANES] - m_ref[...]).astype(BF16)
                 for c in range(MXU_N // LANES)], axis=1)
            vc = jnp.concatenate([v_ref[0, j * tk + lo:j * tk + lo + MXU_N, :], ones], axis=1)
            d = jnp.dot(p, vc, preferred_element_type=F32)
            pv = d if pv is None else pv + d
        alpha = jnp.exp2(m_old - m_new)
        acc_ref[...] = acc_ref[...] * jnp.concatenate([alpha, alpha], axis=1) + pv

    nbuf = s_ref.shape[0]
    s_ref[0] = scores(0)
    for j in range(nk):
        if j + 1 < nk:
            s_ref[(j + 1) % nbuf] = scores(j + 1)
        update(s_ref.at[j % nbuf], j)
    acc = acc_ref[...]
    o1 = acc[:tq, :LANES] / acc[:tq, LANES:]
    o2 = acc[tq:, :LANES] / acc[tq:, LANES:]
    o = o1 - lam_ref[0] * o2
    o = o * lax.rsqrt(jnp.mean(o * o, axis=-1, keepdims=True) + EPS) * g_ref[...] * lam_ref[1]
    o_ref[0] = (o * z_ref[0].astype(F32)).astype(BF16)


def _attn_b(lam, q, k, v, z, g, tq, tk):
    bsz, seq, _ = q.shape
    assert seq % tk == 0 and seq % tq == 0
    qmap = lambda b, h, i: (b, i, h)
    kmap = lambda b, h, i: (b, 0, h)
    return pl.pallas_call(
        functools.partial(_attn_b_kernel, tq=tq, tk=tk, seq=seq),
        grid=(bsz, B_HEADS, seq // tq),
        in_specs=[pl.BlockSpec(memory_space=pltpu.SMEM),
                  pl.BlockSpec((1, tq, LANES), qmap),
                  pl.BlockSpec((1, seq, LANES), kmap),
                  pl.BlockSpec((1, seq, LANES), kmap),
                  pl.BlockSpec((1, tq, LANES), qmap),
                  pl.BlockSpec((1, LANES), lambda b, h, i: (0, 0))],
        out_specs=pl.BlockSpec((1, tq, LANES), qmap),
        out_shape=jax.ShapeDtypeStruct((bsz, seq, B_WIDTH), BF16),
        scratch_shapes=[pltpu.VMEM((2 * tq, 2 * LANES), F32), pltpu.VMEM((2 * tq, LANES), F32),
                        pltpu.VMEM((2, 2 * tq, tk), F32)],
        compiler_params=_cparams(("parallel", "parallel", "parallel")),
    )(lam, q, k, v, z, g)


def _short_conv(xp_ref, x_ref, xn_ref, w_ref, buf_ref, qkv_ref, *, tc, seq):
    pos = (pl.program_id(0) * tc) % seq
    halo = (C_CONV - 1) // 2
    buf_ref[0:SUBLANES, :] = jnp.where(pos == 0, 0.0, xp_ref[...])
    buf_ref[SUBLANES:SUBLANES + tc, :] = x_ref[...]
    buf_ref[SUBLANES + tc:, :] = jnp.where(pos + tc == seq, 0.0, xn_ref[...])
    for j in range(3 * C_HEADS):
        cols = slice(j * LANES, (j + 1) * LANES)
        y = jnp.zeros((tc, LANES), F32)
        for t in range(C_CONV):
            y = y + buf_ref[SUBLANES - halo + t:SUBLANES - halo + t + tc, cols] * w_ref[t:t + 1, cols]
        y = y * jax.nn.sigmoid(y)
        if j < 2 * C_HEADS:
            y = y * lax.rsqrt(jnp.sum(y * y, axis=-1, keepdims=True) + EPS)
        if j < C_HEADS:
            y = y * (C_HEAD_DIM ** -0.5)
        qkv_ref[:, cols] = y


HC = C_HEADS * C_CHUNK
CHUNK_SHIFT = 6


def _mm(a, b):
    return jnp.dot(a, b, preferred_element_type=F32)


def _split3(x):
    hi = x.astype(BF16)
    r = x - hi.astype(F32)
    mid = r.astype(BF16)
    return hi, mid, (r - mid.astype(F32)).astype(BF16)


def _block_diag(x, mask):
    return jnp.where(mask, jnp.concatenate([x] * C_HEADS, axis=0), jnp.zeros((), x.dtype))


def _delta_prep_kernel(xp_ref, x_ref, xn_ref, cw_ref, bg_ref, w_ref, u_ref, qd_ref, ktt_ref, att_ref, eg_ref,
                       buf_ref, qkv_ref, *, rb, seq):
    c = C_CHUNK
    _short_conv(xp_ref, x_ref, xn_ref, cw_ref, buf_ref, qkv_ref, tc=rb, seq=seq)
    bg = bg_ref[...]
    g3 = _split3(bg)
    rr = lax.broadcasted_iota(jnp.int32, (rb, rb), 0)
    cc = lax.broadcasted_iota(jnp.int32, (rb, rb), 1)
    same = (rr >> CHUNK_SHIFT) == (cc >> CHUNK_SHIFT)
    tri = (jnp.where(same & (rr >= cc), 1.0, 0.0).astype(BF16),
           jnp.where(same & (rr <= cc), 1.0, 0.0).astype(BF16))
    gcum = [_mm(t, g3[0]) + _mm(t, g3[1]) + _mm(t, g3[2]) for t in tri]

    row = lax.broadcasted_iota(jnp.int32, (c, HC), 0)
    lm = lax.broadcasted_iota(jnp.int32, (c, HC), 1) & (c - 1)
    eye_cat = row == lm
    lane_lo = lax.broadcasted_iota(jnp.int32, (c, LANES), 1) < c
    bdmask = ((lax.broadcasted_iota(jnp.int32, (HC, HC), 0) >> CHUNK_SHIFT)
              == (lax.broadcasted_iota(jnp.int32, (HC, HC), 1) >> CHUNK_SHIFT))
    zeros_blk = jnp.zeros((c, LANES), BF16)
    heads = range(C_HEADS)

    def col(a, j):
        return jnp.broadcast_to(a[:, j:j + 1], (c, LANES))

    def spread(cols):
        return jnp.concatenate([jnp.where(lane_lo, cols[0], cols[1]),
                                jnp.where(lane_lo, cols[2], cols[3])], axis=1)

    amats = []
    rhss = []
    for ci in range(rb // c):
        rows = slice(ci * c, (ci + 1) * c)
        q = [qkv_ref[rows, h * LANES:(h + 1) * LANES] for h in heads]
        k = [qkv_ref[rows, C_WIDTH + h * LANES:C_WIDTH + (h + 1) * LANES] for h in heads]
        v = [qkv_ref[rows, 2 * C_WIDTH + h * LANES:2 * C_WIDTH + (h + 1) * LANES] for h in heads]
        kb = [t.astype(BF16) for t in k]
        lhs = jnp.concatenate([jnp.concatenate(kb, axis=1),
                               jnp.concatenate([t.astype(BF16) for t in q], axis=1)], axis=0)
        kmask = jnp.concatenate(
            [jnp.concatenate([kb[h] if j == h else zeros_blk for j in heads], axis=1) for h in heads],
            axis=0)
        gram = lax.dot_general(lhs, kmask, (((1,), (1,)), ((), ())), preferred_element_type=F32)
        kk = gram[:c]
        qk = gram[c:]
        beta_all = bg[rows]
        for d in range(2):
            gc = gcum[d][rows]
            off_b = d * C_HEADS
            off_g = 2 * C_HEADS + d * C_HEADS
            bcol = [col(beta_all, off_b + h) for h in heads]
            gcol = [col(gc, off_g + h) for h in heads]
            g_last = gc[c - 1:c, :] if d == 0 else gc[0:1, :]
            glcol = [jnp.broadcast_to(g_last[:, off_g + h:off_g + h + 1], (c, LANES)) for h in heads]
            gc_sp = spread(gcol)
            gc_row = jnp.sum(jnp.where(eye_cat, gc_sp, 0.0), axis=0, keepdims=True)
            incl = (row >= lm) if d == 0 else (row <= lm)
            strict = (row > lm) if d == 0 else (row < lm)
            decay = jnp.where(incl, jnp.exp(jnp.where(incl, gc_sp - gc_row, 0.0)), 0.0)
            amats.append(jnp.where(strict, spread(bcol) * kk * decay, 0.0))
            egc = [jnp.exp(gcol[h]) for h in heads]
            rhss.append(jnp.concatenate(
                [jnp.concatenate([v[h] * bcol[h], k[h] * (bcol[h] * egc[h])], axis=1) for h in heads],
                axis=0))
            qd_ref[ci, d] = jnp.concatenate([q[h] * egc[h] for h in heads], axis=0).astype(BF16)
            kt = jnp.concatenate([k[h] * jnp.exp(glcol[h] - gcol[h]) for h in heads], axis=0)
            ktt_ref[ci, d] = kt.T.astype(BF16)
            att_ref[ci, d] = jnp.where(incl, qk * decay, 0.0).astype(BF16)
            eg_ref[ci, d] = jnp.exp(jnp.concatenate([glcol[h][0:1, :] for h in heads], axis=0))

    units = range(len(amats))
    tms = [-a for a in amats]
    abs_ = [a.astype(BF16) for a in amats]
    pws = [_mm(abs_[i], _block_diag(abs_[i], bdmask)) for i in units]
    for _ in range(CHUNK_SHIFT - 2):
        pbs = [p.astype(BF16) for p in pws]
        both = [_mm(jnp.concatenate([tms[i].astype(BF16), pbs[i]], axis=0), _block_diag(pbs[i], bdmask))
                for i in units]
        tms = [tms[i] + pws[i] + both[i][:c] for i in units]
        pws = [both[i][c:] for i in units]
    last = [_mm(tms[i].astype(BF16), _block_diag(pws[i].astype(BF16), bdmask)) for i in units]
    tms = [tms[i] + pws[i] + last[i] for i in units]
    uws = [_mm(_block_diag(tms[i].astype(BF16), bdmask), rhss[i].astype(BF16)) for i in units]
    for i in units:
        uw = rhss[i] + uws[i]
        u_ref[i // 2, i % 2] = uw[:, :LANES]
        w_ref[i // 2, i % 2] = uw[:, LANES:].astype(BF16)


def _delta_prep(x2, conv_w, bg2, seq, rb):
    rows, width = x2.shape
    nct = rows // C_CHUNK
    nc = rb // C_CHUNK
    r = rb // SUBLANES
    nb8 = rows // SUBLANES
    blk = lambda i: (i, 0, 0, 0)
    shapes = [((HC, LANES), BF16), ((HC, LANES), F32), ((HC, LANES), BF16),
              ((C_HEAD_DIM, HC), BF16), ((C_CHUNK, HC), BF16), ((C_HEADS, LANES), F32)]
    return pl.pallas_call(
        functools.partial(_delta_prep_kernel, rb=rb, seq=seq),
        grid=(rows // rb,),
        in_specs=[pl.BlockSpec((SUBLANES, width), lambda i: (jnp.maximum(i * r - 1, 0), 0)),
                  pl.BlockSpec((rb, width), lambda i: (i, 0)),
                  pl.BlockSpec((SUBLANES, width), lambda i: (jnp.minimum((i + 1) * r, nb8 - 1), 0)),
                  pl.BlockSpec((C_CONV, width), lambda i: (0, 0)),
                  pl.BlockSpec((rb, N_BA), lambda i: (i, 0))],
        out_specs=[pl.BlockSpec((nc, 2) + s, blk) for s, _ in shapes],
        out_shape=[jax.ShapeDtypeStruct((nct, 2) + s, dt) for s, dt in shapes],
        scratch_shapes=[pltpu.VMEM((rb + 2 * SUBLANES, width), F32), pltpu.VMEM((rb, width), F32)],
        compiler_params=_cparams(("parallel",)),
    )(x2, x2, x2, conv_w, bg2)


def _delta_scan_kernel(*refs, cb, bsz):
    ins = (refs[0:6], refs[6:12])
    outs = refs[12:14]
    state_ref = refs[14]
    c = C_CHUNK

    @pl.when(pl.program_id(0) == 0)
    def _():
        state_ref[...] = jnp.zeros_like(state_ref)

    bdmask = ((lax.broadcasted_iota(jnp.int32, (HC, HC), 0) >> CHUNK_SHIFT)
              == (lax.broadcasted_iota(jnp.int32, (HC, HC), 1) >> CHUNK_SHIFT))
    lane_head = lax.broadcasted_iota(jnp.int32, (1, HC), 1) >> CHUNK_SHIFT
    heads = range(C_HEADS)
    chains = [(b, d) for b in range(bsz) for d in range(2)]
    for step in range(cb):
        ci = {d: step if d == 0 else cb - 1 - step for d in range(2)}
        state, prod = {}, {}
        for b, d in chains:
            w = ins[d][0][b, ci[d]]
            qd = ins[d][2][b, ci[d]]
            base = (b * 2 + d) * C_HEADS
            state[b, d] = [state_ref[base + h] for h in heads]
            prod[b, d] = [_mm(jnp.concatenate([w[h * c:(h + 1) * c], qd[h * c:(h + 1) * c]], axis=0),
                              state[b, d][h].astype(BF16)) for h in heads]
        vnb, att_v = {}, {}
        for b, d in chains:
            ws = jnp.concatenate([p[:c] for p in prod[b, d]], axis=0)
            vnb[b, d] = (ins[d][1][b, ci[d]] - ws).astype(BF16)
            att_v[b, d] = _mm(_block_diag(ins[d][4][b, ci[d]], bdmask), vnb[b, d])
        for b, d in chains:
            ktt = ins[d][3][b, ci[d]]
            eg = ins[d][5][b, ci[d]]
            base = (b * 2 + d) * C_HEADS
            for h in heads:
                kth = jnp.where(lane_head == h, ktt, jnp.zeros((), ktt.dtype))
                state_ref[base + h] = state[b, d][h] * eg[h:h + 1, :] + _mm(kth, vnb[b, d])
                outs[d][b, ci[d] * c:(ci[d] + 1) * c, h * LANES:(h + 1) * LANES] = (
                    prod[b, d][h][c:] + att_v[b, d][h * c:(h + 1) * c])


def _delta_scan(prep, bsz, seq, cb):
    nch = seq // C_CHUNK
    nblk = nch // cb
    arrs = [a.reshape((bsz, nch) + a.shape[1:]) for a in prep]

    def specs(d):
        idx = (lambda n: (0, n, d, 0, 0)) if d == 0 else (lambda n: (0, nblk - 1 - n, d, 0, 0))
        return [pl.BlockSpec((bsz, cb, None) + a.shape[3:], idx) for a in arrs]

    out = jax.ShapeDtypeStruct((bsz, seq, C_WIDTH), F32)
    return pl.pallas_call(
        functools.partial(_delta_scan_kernel, cb=cb, bsz=bsz),
        grid=(nblk,),
        in_specs=specs(0) + specs(1),
        out_specs=[pl.BlockSpec((bsz, cb * C_CHUNK, C_WIDTH), lambda n: (0, n, 0)),
                   pl.BlockSpec((bsz, cb * C_CHUNK, C_WIDTH), lambda n: (0, nblk - 1 - n, 0))],
        out_shape=[out, out],
        scratch_shapes=[pltpu.VMEM((bsz * 2 * C_HEADS, C_HEAD_DIM, C_HEAD_DIM), F32)],
        compiler_params=_cparams(("arbitrary",)),
    )(*arrs, *arrs)


def _merge_kernel(x_ref, g_ref, wl_ref, oa_ref, ob_ref, ocf_ref, ocb_ref, cg_ref, wa_ref, wb_ref, wc_ref,
                  wo_ref, fg_ref, o_ref, zc_ref, gates_ref, *, final):
    hb = _normed_bf16(x_ref[...], g_ref[...])
    slabs = ([(zc_ref, j, _silu) for j in range(C_WIDTH // LANES)]
             + [(gates_ref, j, jax.nn.sigmoid) for j in range(N_BRANCH * D_MODEL // LANES)])
    _project(hb, wl_ref, slabs)
    oc = ocf_ref[...] + ocb_ref[...]
    cg = cg_ref[...]
    parts = []
    for h in range(C_HEADS):
        blk = oc[:, h * LANES:(h + 1) * LANES]
        blk = blk * lax.rsqrt(jnp.mean(blk * blk, axis=-1, keepdims=True) + EPS) * cg
        parts.append((blk * zc_ref[:, h * LANES:(h + 1) * LANES].astype(F32)).astype(BF16))
    ocn = jnp.concatenate(parts, axis=1)
    ya = jnp.dot(oa_ref[...], wa_ref[...], preferred_element_type=F32)
    yb = jnp.dot(ob_ref[...], wb_ref[...], preferred_element_type=F32)
    yc = jnp.dot(ocn, wc_ref[...], preferred_element_type=F32)
    d = D_MODEL
    merged = (gates_ref[:, 0:d].astype(F32) * ya + gates_ref[:, d:2 * d].astype(F32) * yb
              + gates_ref[:, 2 * d:3 * d].astype(F32) * yc)
    x = x_ref[...] + jnp.dot(merged.astype(BF16), wo_ref[...], preferred_element_type=F32)
    if final:
        x = x * lax.rsqrt(jnp.mean(x * x, axis=-1, keepdims=True) + EPS) * fg_ref[...]
    o_ref[...] = x


def _merge(x2, g, wl, oa, ob, ocf, ocb, cg, wa, wb, wc, wo, fg, tm, final):
    rows = x2.shape[0]
    row = lambda i: (i, 0)
    const = lambda i: (0, 0)
    return pl.pallas_call(
        functools.partial(_merge_kernel, final=final),
        grid=(rows // tm,),
        in_specs=[pl.BlockSpec((tm, D_MODEL), row),
                  pl.BlockSpec((1, D_MODEL), const),
                  pl.BlockSpec((D_MODEL, N_LATE), const),
                  pl.BlockSpec((tm, A_WIDTH), row), pl.BlockSpec((tm, B_WIDTH), row),
                  pl.BlockSpec((tm, C_WIDTH), row), pl.BlockSpec((tm, C_WIDTH), row),
                  pl.BlockSpec((1, LANES), const),
                  pl.BlockSpec((A_WIDTH, D_MODEL), const), pl.BlockSpec((B_WIDTH, D_MODEL), const),
                  pl.BlockSpec((C_WIDTH, D_MODEL), const), pl.BlockSpec((D_MODEL, D_MODEL), const),
                  pl.BlockSpec((1, D_MODEL), const)],
        out_specs=pl.BlockSpec((tm, D_MODEL), row),
        out_shape=jax.ShapeDtypeStruct((rows, D_MODEL), F32),
        scratch_shapes=[pltpu.VMEM((tm, C_WIDTH), BF16), pltpu.VMEM((tm, N_BRANCH * D_MODEL), BF16)],
        compiler_params=_cparams(("parallel",)),
    )(x2, g, wl, oa, ob, ocf, ocb, cg, wa, wb, wc, wo, fg)


def _rope_tables(seq):
    inv = 1.0 / (ROPE_THETA ** (jnp.arange(0, HEAD_DIM, 2, dtype=F32) / HEAD_DIM))
    ang = jnp.arange(seq, dtype=F32)[:, None] * inv[None, :]
    cos = jnp.tile(jnp.cos(ang), (1, LANES // (HEAD_DIM // 2)))
    sin = jnp.tile(jnp.concatenate([-jnp.sin(ang), jnp.sin(ang)], axis=1), (1, LANES // HEAD_DIM))
    return cos, sin


def kernel(x, norm_g, w_in, a_sink, b_lambda, b_subln_g, c_conv_w, c_a_log, c_dt_bias, c_norm_g,
           w_bo_a, w_bo_b, w_bo_c, w_out, final_g):
    bsz, seq, _ = x.shape
    depth = w_in.shape[0]
    rows = bsz * seq
    tm = min(512, seq)
    tq_a = min(256, seq)
    tq_b = min(512, seq)
    tk_b = min(1024, seq)
    cos, sin = _rope_tables(seq)

    def pair_heads(t, axis):
        shp = t.shape
        t = t.reshape(shp[:axis] + (A_KV_HEADS, A_Q_HEADS // A_KV_HEADS, HEAD_DIM) + shp[axis + 1:])
        return jnp.swapaxes(t, axis, axis + 1).reshape(shp)

    w_in16 = lax.optimization_barrier(w_in.astype(BF16))

    def sec(i, scale=None):
        t = w_in16[:, :, _OFF[i]:_OFF[i + 1]]
        return t if scale is None else t * jnp.asarray(scale, BF16)

    qscale = HEAD_DIM ** -0.5
    w_main = jnp.concatenate(
        [pair_heads(sec(0, qscale), 2), sec(1), sec(2), pair_heads(sec(3), 2), sec(4, qscale),
         sec(5), sec(6), sec(7), sec(8)], axis=2)
    w_late = jnp.concatenate([sec(9), sec(12)], axis=2)
    w_ba = w_in16[:, :, _OFF[10]:_OFF[12]]
    dec = jnp.stack([jnp.concatenate([jnp.zeros((depth, 2 * C_HEADS), F32),
                                      -jnp.exp(c_a_log.astype(F32)).reshape(depth, 2 * C_HEADS)], axis=1),
                     jnp.concatenate([jnp.zeros((depth, 2 * C_HEADS), F32),
                                      c_dt_bias.astype(F32).reshape(depth, 2 * C_HEADS)], axis=1)], axis=1)
    wa = pair_heads(w_bo_a, 1).astype(BF16)
    wb = w_bo_b.astype(BF16)
    wc = w_bo_c.astype(BF16)
    wo = w_out.astype(BF16)

    x2 = x.reshape(rows, D_MODEL)
    for l in range(depth):
        (qa, ka, va, za, qb, kb, vb, zb, cqkv, bg) = _inproj(
            x2, norm_g[l][None], w_main[l], w_ba[l], cos, sin, dec[l], seq, tm)

        r3 = lambda t: t.reshape(bsz, seq, t.shape[-1])
        oa = _attn_a(a_sink[l].astype(F32), r3(qa), r3(ka), r3(va), r3(za), tq_a)

        lam_init = 0.8 - 0.6 * math.exp(-0.3 * l)
        bl = b_lambda[l].astype(F32)
        lam = jnp.exp(jnp.sum(bl[0] * bl[1])) - jnp.exp(jnp.sum(bl[2] * bl[3])) + lam_init
        lam_s = jnp.stack([lam, jnp.asarray(1.0 - lam_init, F32)])
        ob = _attn_b(lam_s, r3(qb), r3(kb), r3(vb), r3(zb), b_subln_g[l][None].astype(F32), tq_b, tk_b)

        prep = _delta_prep(cqkv, c_conv_w[l].astype(F32), bg, seq, min(256, seq))
        ocf, ocb = _delta_scan(prep, bsz, seq, 2)

        x2 = _merge(x2, norm_g[l][None], w_late[l], oa.reshape(rows, A_WIDTH), ob.reshape(rows, B_WIDTH),
                    ocf.reshape(rows, C_WIDTH), ocb.reshape(rows, C_WIDTH),
                    c_norm_g[l][None].astype(F32), wa[l], wb[l], wc[l], wo[l],
                    final_g[None].astype(F32), tm, l == depth - 1)
    return x2.reshape(bsz, seq, D_MODEL)
```

```python
import functools
import math

import numpy as np
import jax
import jax.numpy as jnp
from jax import lax
from jax.experimental import pallas as pl
from jax.experimental.pallas import tpu as pltpu

F32 = jnp.float32
BF16 = jnp.bfloat16

D_MODEL = 1024
HEAD_DIM = 64
ROPE_THETA = 10000.0
EPS = 1e-6
A_Q_HEADS = 8
A_KV_HEADS = 2
A_WIDTH = A_Q_HEADS * HEAD_DIM
A_KV_WIDTH = A_KV_HEADS * HEAD_DIM
WINDOW = 128
B_HEADS = 4
B_V_DIM = 2 * HEAD_DIM
B_WIDTH = B_HEADS * B_V_DIM
C_HEADS = 4
C_HEAD_DIM = 128
C_WIDTH = C_HEADS * C_HEAD_DIM
C_CONV = 5
C_CHUNK = 64
N_BRANCH = 3
LOG2E = math.log2(math.e)
LANES = 128
SUBLANES = 8
MXU_N = 256
VMEM_LIMIT = 56 * 1024 * 1024

IN_SPLITS = (A_WIDTH, A_KV_WIDTH, A_KV_WIDTH, A_WIDTH,
             B_WIDTH, B_WIDTH, B_WIDTH, B_WIDTH,
             3 * C_WIDTH, C_WIDTH, 2 * C_HEADS, 2 * C_HEADS,
             N_BRANCH * D_MODEL)
_OFF = [0] + [int(o) for o in np.cumsum(IN_SPLITS)]

_SEC = {}
for _name, _width in (('aq', A_WIDTH), ('ak', A_KV_WIDTH), ('av', A_KV_WIDTH), ('az', A_WIDTH),
                      ('bq', B_WIDTH), ('bk', B_WIDTH), ('bv', B_WIDTH), ('bz', B_WIDTH),
                      ('cqkv', 3 * C_WIDTH)):
    _SEC[_name] = (sum(w for _, w in _SEC.values()), _width)
N_PROJ = sum(w for _, w in _SEC.values())
N_BA = 4 * C_HEADS
N_LATE = C_WIDTH + N_BRANCH * D_MODEL


def _cparams(sem):
    return pltpu.CompilerParams(dimension_semantics=sem, vmem_limit_bytes=VMEM_LIMIT)


def _rope(t, cos, sin_signed, first_half):
    swapped = jnp.where(first_half, pltpu.roll(t, LANES - 32, 1), pltpu.roll(t, 32, 1))
    return t * cos + swapped * sin_signed


def _softplus(t):
    return jnp.maximum(t, 0.0) + jnp.log(1.0 + jnp.exp(-jnp.abs(t)))


def _normed_bf16(x, g):
    return (x * lax.rsqrt(jnp.mean(x * x, axis=-1, keepdims=True) + EPS) * g).astype(BF16)


def _silu(t):
    return t * jax.nn.sigmoid(t)


def _project(hb, w_ref, slabs):
    for pair in range(len(slabs) // 2):
        acc = jnp.dot(hb, w_ref[:, pair * MXU_N:(pair + 1) * MXU_N], preferred_element_type=F32)
        for half in range(2):
            ref, j, fn = slabs[2 * pair + half]
            t = acc[:, half * LANES:(half + 1) * LANES]
            ref[:, j * LANES:(j + 1) * LANES] = (t if fn is None else fn(t)).astype(ref.dtype)


def _inproj_kernel(x_ref, g_ref, w_ref, wba_ref, cos_ref, sin_ref, dec_ref,
                   qa_ref, ka_ref, va_ref, za_ref, qb_ref, kb_ref, vb_ref, zb_ref, cqkv_ref, bg_ref):
    hb = _normed_bf16(x_ref[...], g_ref[...])
    cos = cos_ref[...]
    sin = sin_ref[...]
    first_half = (lax.broadcasted_iota(jnp.int32, cos.shape, 1) & 32) == 0

    def rope(t):
        return _rope(t, cos, sin, first_half)

    def rope_q(t):
        return rope(t) * LOG2E

    epilogues = (('aq', qa_ref, rope_q), ('ak', ka_ref, rope), ('av', va_ref, None), ('az', za_ref, _silu),
                 ('bq', qb_ref, rope_q), ('bk', kb_ref, rope), ('bv', vb_ref, None),
                 ('bz', zb_ref, _silu), ('cqkv', cqkv_ref, None))
    slabs = []
    for name, ref, fn in epilogues:
        assert _SEC[name][0] == len(slabs) * LANES
        slabs += [(ref, j, fn) for j in range(_SEC[name][1] // LANES)]
    _project(hb, w_ref, slabs)

    ba = jnp.dot(hb, wba_ref[...], preferred_element_type=F32)
    is_beta = lax.broadcasted_iota(jnp.int32, ba.shape, 1) < 2 * C_HEADS
    dec = dec_ref[...]
    bg_ref[...] = jnp.where(is_beta, jax.nn.sigmoid(ba), dec[0:1, :] * _softplus(ba + dec[1:2, :]))


def _inproj(x2, g, w, wba, cos, sin, dec, seq, tm):
    rows = x2.shape[0]
    nseq = seq // tm
    row = lambda i: (i, 0)
    const = lambda i: (0, 0)
    single = pl.Buffered(1)
    out_widths = [('aq', BF16), ('ak', BF16), ('av', BF16), ('az', BF16), ('bq', BF16), ('bk', BF16),
                  ('bv', BF16), ('bz', BF16), ('cqkv', F32)]
    out_shape = [jax.ShapeDtypeStruct((rows, _SEC[n][1]), dt) for n, dt in out_widths]
    out_specs = [pl.BlockSpec((tm, _SEC[n][1]), row) for n, _ in out_widths]
    out_shape += [jax.ShapeDtypeStruct((rows, N_BA), F32)]
    out_specs += [pl.BlockSpec((tm, N_BA), row)]
    return pl.pallas_call(
        _inproj_kernel,
        grid=(rows // tm,),
        in_specs=[pl.BlockSpec((tm, D_MODEL), row),
                  pl.BlockSpec((1, D_MODEL), const),
                  pl.BlockSpec((D_MODEL, N_PROJ), const, pipeline_mode=single),
                  pl.BlockSpec((D_MODEL, N_BA), const),
                  pl.BlockSpec((tm, LANES), lambda i: (i % nseq, 0)),
                  pl.BlockSpec((tm, LANES), lambda i: (i % nseq, 0)),
                  pl.BlockSpec((2, N_BA), const)],
        out_specs=out_specs,
        out_shape=out_shape,
        compiler_params=_cparams(("parallel",)),
    )(x2, g, w, wba, cos, sin, dec)


def _attn_a_kernel(sink_ref, q_ref, kp_ref, k_ref, kn_ref, vp_ref, v_ref, vn_ref, z_ref, o_ref, *, tq, seq):
    i = pl.program_id(1)
    kcat = jnp.concatenate([kp_ref[0], k_ref[0], kn_ref[0]], axis=0)
    vcat = jnp.concatenate([vp_ref[0], v_ref[0], vn_ref[0]], axis=0)
    lane = lax.broadcasted_iota(jnp.int32, (1, LANES), 1)
    low = lane < HEAD_DIM
    one = jnp.ones_like(vcat)
    v_lo = jnp.where(low, vcat, one)
    v_hi = jnp.where(low, one, vcat)
    nk = tq + 2 * WINDOW
    r = lax.broadcasted_iota(jnp.int32, (tq, nk), 0)
    c = lax.broadcasted_iota(jnp.int32, (tq, nk), 1)
    kabs = i * tq - WINDOW + c
    valid = (jnp.abs(c - WINDOW - r) <= WINDOW) & (kabs >= 0) & (kabs < seq)

    def probs(qm, sink):
        s = lax.dot_general(qm, kcat, (((1,), (1,)), ((), ())), preferred_element_type=F32)
        s = jnp.where(valid, s, -1e30)
        m = jnp.maximum(jnp.max(s, axis=-1, keepdims=True), sink)
        return jnp.exp2(s - m).astype(BF16), jnp.exp2(sink - m)

    half = A_Q_HEADS // 2
    for b in range(half):
        q = q_ref[0, :, b * LANES:(b + 1) * LANES]
        p0, e0 = probs(jnp.where(low, q, jnp.zeros_like(q)), sink_ref[b] * LOG2E)
        p1, e1 = probs(jnp.where(low, jnp.zeros_like(q), q), sink_ref[half + b] * LOG2E)
        r0 = jnp.dot(p0, v_lo, preferred_element_type=F32)
        r1 = jnp.dot(p1, v_hi, preferred_element_type=F32)
        num = jnp.where(low, r0, r1)
        den = (jnp.where(low, pltpu.roll(r0, HEAD_DIM, 1), pltpu.roll(r1, HEAD_DIM, 1))
               + jnp.where(low, e0, e1))
        o_ref[0, :, b * LANES:(b + 1) * LANES] = (
            num / den * z_ref[0, :, b * LANES:(b + 1) * LANES].astype(F32)).astype(BF16)


def _attn_a(sink, q, k, v, z, tq):
    bsz, seq, _ = q.shape
    r = tq // WINDOW
    nwb = seq // WINDOW
    main = lambda b, i: (b, i, 0)
    prev = lambda b, i: (b, jnp.maximum(i * r - 1, 0), 0)
    nxt = lambda b, i: (b, jnp.minimum((i + 1) * r, nwb - 1), 0)
    kv_specs = [pl.BlockSpec((1, WINDOW, LANES), prev), pl.BlockSpec((1, tq, LANES), main),
                pl.BlockSpec((1, WINDOW, LANES), nxt)]
    return pl.pallas_call(
        functools.partial(_attn_a_kernel, tq=tq, seq=seq),
        grid=(bsz, seq // tq),
        in_specs=[pl.BlockSpec(memory_space=pltpu.SMEM), pl.BlockSpec((1, tq, A_WIDTH), main)]
                 + kv_specs + kv_specs + [pl.BlockSpec((1, tq, A_WIDTH), main)],
        out_specs=pl.BlockSpec((1, tq, A_WIDTH), main),
        out_shape=jax.ShapeDtypeStruct((bsz, seq, A_WIDTH), BF16),
        compiler_params=_cparams(("parallel", "parallel")),
    )(sink, q, k, k, k, v, v, v, z)


def _attn_b_kernel(lam_ref, q_ref, k_ref, v_ref, z_ref, g_ref, o_ref, acc_ref, m_ref, s_ref,
                   *, tq, tk, seq):
    q = q_ref[0]
    low = lax.broadcasted_iota(jnp.int32, (1, LANES), 1) < HEAD_DIM
    zero = jnp.zeros_like(q)
    q2 = jnp.concatenate([jnp.where(low, q, zero), jnp.where(low, zero, q)], axis=0)
    acc_ref[...] = jnp.zeros_like(acc_ref)
    m_ref[...] = jnp.full_like(m_ref, -jnp.inf)
    ones = jnp.ones((MXU_N, LANES), BF16)
    nk = seq // tk

    def scores(j):
        return lax.dot_general(q2, k_ref[0, j * tk:(j + 1) * tk, :], (((1,), (1,)), ((), ())),
                               preferred_element_type=F32)

    def update(s_ref, j):
        m_old = m_ref[...]
        m_new = jnp.maximum(m_old, jnp.max(s_ref[...], axis=-1, keepdims=True))
        m_ref[...] = m_new
        pv = None
        for kb in range(tk // MXU_N):
            lo = kb * MXU_N
            p = jnp.concatenate(
                [jnp.exp2(s_ref[:, lo + c * LANES:lo + (c + 1) * LANES] - m_ref[...]).astype(BF16)
                 for c in range(MXU_N // LANES)], axis=1)
            vc = jnp.concatenate([v_ref[0, j * tk + lo:j * tk + lo + MXU_N, :], ones], axis=1)
            d = jnp.dot(p, vc, preferred_element_type=F32)
            pv = d if pv is None else pv + d
        alpha = jnp.exp2(m_old - m_new)
        acc_ref[...] = acc_ref[...] * jnp.concatenate([alpha, alpha], axis=1) + pv

    nbuf = s_ref.shape[0]
    s_ref[0] = scores(0)
    for j in range(nk):
        if j + 1 < nk:
            s_ref[(j + 1) % nbuf] = scores(j + 1)
        update(s_ref.at[j % nbuf], j)
    acc = acc_ref[...]
    o1 = acc[:tq, :LANES] / acc[:tq, LANES:]
    o2 = acc[tq:, :LANES] / acc[tq:, LANES:]
    o = o1 - lam_ref[0] * o2
    o = o * lax.rsqrt(jnp.mean(o * o, axis=-1, keepdims=True) + EPS) * g_ref[...] * lam_ref[1]
    o_ref[0] = (o * z_ref[0].astype(F32)).astype(BF16)


def _attn_b(lam, q, k, v, z, g, tq, tk):
    bsz, seq, _ = q.shape
    assert seq % tk == 0 and seq % tq == 0
    qmap = lambda b, h, i: (b, i, h)
    kmap = lambda b, h, i: (b, 0, h)
    return pl.pallas_call(
        functools.partial(_attn_b_kernel, tq=tq, tk=tk, seq=seq),
        grid=(bsz, B_HEADS, seq // tq),
        in_specs=[pl.BlockSpec(memory_space=pltpu.SMEM),
                  pl.BlockSpec((1, tq, LANES), qmap),
                  pl.BlockSpec((1, seq, LANES), kmap),
                  pl.BlockSpec((1, seq, LANES), kmap),
                  pl.BlockSpec((1, tq, LANES), qmap),
                  pl.BlockSpec((1, LANES), lambda b, h, i: (0, 0))],
        out_specs=pl.BlockSpec((1, tq, LANES), qmap),
        out_shape=jax.ShapeDtypeStruct((bsz, seq, B_WIDTH), BF16),
        scratch_shapes=[pltpu.VMEM((2 * tq, 2 * LANES), F32), pltpu.VMEM((2 * tq, LANES), F32),
                        pltpu.VMEM((2, 2 * tq, tk), F32)],
        compiler_params=_cparams(("parallel", "parallel", "parallel")),
    )(lam, q, k, v, z, g)


def _short_conv(xp_ref, x_ref, xn_ref, w_ref, buf_ref, qkv_ref, *, tc, seq):
    pos = (pl.program_id(0) * tc) % seq
    halo = (C_CONV - 1) // 2
    buf_ref[0:SUBLANES, :] = jnp.where(pos == 0, 0.0, xp_ref[...])
    buf_ref[SUBLANES:SUBLANES + tc, :] = x_ref[...]
    buf_ref[SUBLANES + tc:, :] = jnp.where(pos + tc == seq, 0.0, xn_ref[...])
    for j in range(3 * C_HEADS):
        cols = slice(j * LANES, (j + 1) * LANES)
        y = jnp.zeros((tc, LANES), F32)
        for t in range(C_CONV):
            y = y + buf_ref[SUBLANES - halo + t:SUBLANES - halo + t + tc, cols] * w_ref[t:t + 1, cols]
        y = y * jax.nn.sigmoid(y)
        if j < 2 * C_HEADS:
            y = y * lax.rsqrt(jnp.sum(y * y, axis=-1, keepdims=True) + EPS)
        if j < C_HEADS:
            y = y * (C_HEAD_DIM ** -0.5)
        qkv_ref[:, cols] = y


HC = C_HEADS * C_CHUNK
CHUNK_SHIFT = 6


def _mm(a, b):
    return jnp.dot(a, b, preferred_element_type=F32)


def _split3(x):
    hi = x.astype(BF16)
    r = x - hi.astype(F32)
    mid = r.astype(BF16)
    return hi, mid, (r - mid.astype(F32)).astype(BF16)


def _block_diag(x, mask):
    return jnp.where(mask, jnp.concatenate([x] * C_HEADS, axis=0), jnp.zeros((), x.dtype))


def _delta_prep_kernel(xp_ref, x_ref, xn_ref, cw_ref, bg_ref, w_ref, u_ref, qd_ref, ktt_ref, att_ref, eg_ref,
                       buf_ref, qkv_ref, *, rb, seq):
    c = C_CHUNK
    _short_conv(xp_ref, x_ref, xn_ref, cw_ref, buf_ref, qkv_ref, tc=rb, seq=seq)
    bg = bg_ref[...]
    g3 = _split3(bg)
    rr = lax.broadcasted_iota(jnp.int32, (rb, rb), 0)
    cc = lax.broadcasted_iota(jnp.int32, (rb, rb), 1)
    same = (rr >> CHUNK_SHIFT) == (cc >> CHUNK_SHIFT)
    tri = (jnp.where(same & (rr >= cc), 1.0, 0.0).astype(BF16),
           jnp.where(same & (rr <= cc), 1.0, 0.0).astype(BF16))
    gcum = [_mm(t, g3[0]) + _mm(t, g3[1]) + _mm(t, g3[2]) for t in tri]

    row = lax.broadcasted_iota(jnp.int32, (c, HC), 0)
    lm = lax.broadcasted_iota(jnp.int32, (c, HC), 1) & (c - 1)
    eye_cat = row == lm
    lane_lo = lax.broadcasted_iota(jnp.int32, (c, LANES), 1) < c
    bdmask = ((lax.broadcasted_iota(jnp.int32, (HC, HC), 0) >> CHUNK_SHIFT)
              == (lax.broadcasted_iota(jnp.int32, (HC, HC), 1) >> CHUNK_SHIFT))
    zeros_blk = jnp.zeros((c, LANES), BF16)
    heads = range(C_HEADS)

    def col(a, j):
        return jnp.broadcast_to(a[:, j:j + 1], (c, LANES))

    def spread(cols):
        return jnp.concatenate([jnp.where(lane_lo, cols[0], cols[1]),
                                jnp.where(lane_lo, cols[2], cols[3])], axis=1)

    amats = []
    rhss = []
    for ci in range(rb // c):
        rows = slice(ci * c, (ci + 1) * c)
        q = [qkv_ref[rows, h * LANES:(h + 1) * LANES] for h in heads]
        k = [qkv_ref[rows, C_WIDTH + h * LANES:C_WIDTH + (h + 1) * LANES] for h in heads]
        v = [qkv_ref[rows, 2 * C_WIDTH + h * LANES:2 * C_WIDTH + (h + 1) * LANES] for h in heads]
        kb = [t.astype(BF16) for t in k]
        lhs = jnp.concatenate([jnp.concatenate(kb, axis=1),
                               jnp.concatenate([t.astype(BF16) for t in q], axis=1)], axis=0)
        kmask = jnp.concatenate(
            [jnp.concatenate([kb[h] if j == h else zeros_blk for j in heads], axis=1) for h in heads],
            axis=0)
        gram = lax.dot_general(lhs, kmask, (((1,), (1,)), ((), ())), preferred_element_type=F32)
        kk = gram[:c]
        qk = gram[c:]
        beta_all = bg[rows]
        for d in range(2):
            gc = gcum[d][rows]
            off_b = d * C_HEADS
            off_g = 2 * C_HEADS + d * C_HEADS
            bcol = [col(beta_all, off_b + h) for h in heads]
            gcol = [col(gc, off_g + h) for h in heads]
            g_last = gc[c - 1:c, :] if d == 0 else gc[0:1, :]
            glcol = [jnp.broadcast_to(g_last[:, off_g + h:off_g + h + 1], (c, LANES)) for h in heads]
            gc_sp = spread(gcol)
            gc_row = jnp.sum(jnp.where(eye_cat, gc_sp, 0.0), axis=0, keepdims=True)
            incl = (row >= lm) if d == 0 else (row <= lm)
            strict = (row > lm) if d == 0 else (row < lm)
            decay = jnp.where(incl, jnp.exp(jnp.where(incl, gc_sp - gc_row, 0.0)), 0.0)
            amats.append(jnp.where(strict, spread(bcol) * kk * decay, 0.0))
            egc = [jnp.exp(gcol[h]) for h in heads]
            rhss.append(jnp.concatenate(
                [jnp.concatenate([v[h] * bcol[h], k[h] * (bcol[h] * egc[h])], axis=1) for h in heads],
                axis=0))
            qd_ref[ci, d] = jnp.concatenate([q[h] * egc[h] for h in heads], axis=0).astype(BF16)
            kt = jnp.concatenate([k[h] * jnp.exp(glcol[h] - gcol[h]) for h in heads], axis=0)
            ktt_ref[ci, d] = kt.T.astype(BF16)
            att_ref[ci, d] = jnp.where(incl, qk * decay, 0.0).astype(BF16)
            eg_ref[ci, d] = jnp.exp(jnp.concatenate([glcol[h][0:1, :] for h in heads], axis=0))

    units = range(len(amats))
    tms = [-a for a in amats]
    abs_ = [a.astype(BF16) for a in amats]
    pws = [_mm(abs_[i], _block_diag(abs_[i], bdmask)) for i in units]
    for _ in range(CHUNK_SHIFT - 2):
        pbs = [p.astype(BF16) for p in pws]
        both = [_mm(jnp.concatenate([tms[i].astype(BF16), pbs[i]], axis=0), _block_diag(pbs[i], bdmask))
                for i in units]
        tms = [tms[i] + pws[i] + both[i][:c] for i in units]
        pws = [both[i][c:] for i in units]
    last = [_mm(tms[i].astype(BF16), _block_diag(pws[i].astype(BF16), bdmask)) for i in units]
    tms = [tms[i] + pws[i] + last[i] for i in units]
    uws = [_mm(_block_diag(tms[i].astype(BF16), bdmask), rhss[i].astype(BF16)) for i in units]
    for i in units:
        uw = rhss[i] + uws[i]
        u_ref[i // 2, i % 2] = uw[:, :LANES]
        w_ref[i // 2, i % 2] = uw[:, LANES:].astype(BF16)


def _delta_prep(x2, conv_w, bg2, seq, rb):
    rows, width = x2.shape
    nct = rows // C_CHUNK
    nc = rb // C_CHUNK
    r = rb // SUBLANES
    nb8 = rows // SUBLANES
    blk = lambda i: (i, 0, 0, 0)
    shapes = [((HC, LANES), BF16), ((HC, LANES), F32), ((HC, LANES), BF16),
              ((C_HEAD_DIM, HC), BF16), ((C_CHUNK, HC), BF16), ((C_HEADS, LANES), F32)]
    return pl.pallas_call(
        functools.partial(_delta_prep_kernel, rb=rb, seq=seq),
        grid=(rows // rb,),
        in_specs=[pl.BlockSpec((SUBLANES, width), lambda i: (jnp.maximum(i * r - 1, 0), 0)),
                  pl.BlockSpec((rb, width), lambda i: (i, 0)),
                  pl.BlockSpec((SUBLANES, width), lambda i: (jnp.minimum((i + 1) * r, nb8 - 1), 0)),
                  pl.BlockSpec((C_CONV, width), lambda i: (0, 0)),
                  pl.BlockSpec((rb, N_BA), lambda i: (i, 0))],
        out_specs=[pl.BlockSpec((nc, 2) + s, blk) for s, _ in shapes],
        out_shape=[jax.ShapeDtypeStruct((nct, 2) + s, dt) for s, dt in shapes],
        scratch_shapes=[pltpu.VMEM((rb + 2 * SUBLANES, width), F32), pltpu.VMEM((rb, width), F32)],
        compiler_params=_cparams(("parallel",)),
    )(x2, x2, x2, conv_w, bg2)


def _delta_scan_kernel(*refs, cb, bsz):
    ins = (refs[0:6], refs[6:12])
    outs = refs[12:14]
    state_ref = refs[14]
    c = C_CHUNK

    @pl.when(pl.program_id(0) == 0)
    def _():
        state_ref[...] = jnp.zeros_like(state_ref)

    bdmask = ((lax.broadcasted_iota(jnp.int32, (HC, HC), 0) >> CHUNK_SHIFT)
              == (lax.broadcasted_iota(jnp.int32, (HC, HC), 1) >> CHUNK_SHIFT))
    lane_head = lax.broadcasted_iota(jnp.int32, (1, HC), 1) >> CHUNK_SHIFT
    heads = range(C_HEADS)
    chains = [(b, d) for b in range(bsz) for d in range(2)]
    for step in range(cb):
        ci = {d: step if d == 0 else cb - 1 - step for d in range(2)}
        state, prod = {}, {}
        for b, d in chains:
            w = ins[d][0][b, ci[d]]
            qd = ins[d][2][b, ci[d]]
            base = (b * 2 + d) * C_HEADS
            state[b, d] = [state_ref[base + h] for h in heads]
            prod[b, d] = [_mm(jnp.concatenate([w[h * c:(h + 1) * c], qd[h * c:(h + 1) * c]], axis=0),
                              state[b, d][h].astype(BF16)) for h in heads]
        vnb, att_v = {}, {}
        for b, d in chains:
            ws = jnp.concatenate([p[:c] for p in prod[b, d]], axis=0)
            vnb[b, d] = (ins[d][1][b, ci[d]] - ws).astype(BF16)
            att_v[b, d] = _mm(_block_diag(ins[d][4][b, ci[d]], bdmask), vnb[b, d])
        for b, d in chains:
            ktt = ins[d][3][b, ci[d]]
            eg = ins[d][5][b, ci[d]]
            base = (b * 2 + d) * C_HEADS
            for h in heads:
                kth = jnp.where(lane_head == h, ktt, jnp.zeros((), ktt.dtype))
                state_ref[base + h] = state[b, d][h] * eg[h:h + 1, :] + _mm(kth, vnb[b, d])
                outs[d][b, ci[d] * c:(ci[d] + 1) * c, h * LANES:(h + 1) * LANES] = (
                    prod[b, d][h][c:] + att_v[b, d][h * c:(h + 1) * c])


def _delta_scan(prep, bsz, seq, cb):
    nch = seq // C_CHUNK
    nblk = nch // cb
    arrs = [a.reshape((bsz, nch) + a.shape[1:]) for a in prep]

    def specs(d):
        idx = (lambda n: (0, n, d, 0, 0)) if d == 0 else (lambda n: (0, nblk - 1 - n, d, 0, 0))
        return [pl.BlockSpec((bsz, cb, None) + a.shape[3:], idx) for a in arrs]

    out = jax.ShapeDtypeStruct((bsz, seq, C_WIDTH), F32)
    return pl.pallas_call(
        functools.partial(_delta_scan_kernel, cb=cb, bsz=bsz),
        grid=(nblk,),
        in_specs=specs(0) + specs(1),
        out_specs=[pl.BlockSpec((bsz, cb * C_CHUNK, C_WIDTH), lambda n: (0, n, 0)),
                   pl.BlockSpec((bsz, cb * C_CHUNK, C_WIDTH), lambda n: (0, nblk - 1 - n, 0))],
        out_shape=[out, out],
        scratch_shapes=[pltpu.VMEM((bsz * 2 * C_HEADS, C_HEAD_DIM, C_HEAD_DIM), F32)],
        compiler_params=_cparams(("arbitrary",)),
    )(*arrs, *arrs)


def _merge_kernel(x_ref, g_ref, wl_ref, oa_ref, ob_ref, ocf_ref, ocb_ref, cg_ref, wa_ref, wb_ref, wc_ref,
                  wo_ref, fg_ref, o_ref, zc_ref, gates_ref, *, final):
    hb = _normed_bf16(x_ref[...], g_ref[...])
    slabs = ([(zc_ref, j, _silu) for j in range(C_WIDTH // LANES)]
             + [(gates_ref, j, jax.nn.sigmoid) for j in range(N_BRANCH * D_MODEL // LANES)])
    _project(hb, wl_ref, slabs)
    oc = ocf_ref[...] + ocb_ref[...]
    cg = cg_ref[...]
    parts = []
    for h in range(C_HEADS):
        blk = oc[:, h * LANES:(h + 1) * LANES]
        blk = blk * lax.rsqrt(jnp.mean(blk * blk, axis=-1, keepdims=True) + EPS) * cg
        parts.append((blk * zc_ref[:, h * LANES:(h + 1) * LANES].astype(F32)).astype(BF16))
    ocn = jnp.concatenate(parts, axis=1)
    ya = jnp.dot(oa_ref[...], wa_ref[...], preferred_element_type=F32)
    yb = jnp.dot(ob_ref[...], wb_ref[...], preferred_element_type=F32)
    yc = jnp.dot(ocn, wc_ref[...], preferred_element_type=F32)
    d = D_MODEL
    merged = (gates_ref[:, 0:d].astype(F32) * ya + gates_ref[:, d:2 * d].astype(F32) * yb
              + gates_ref[:, 2 * d:3 * d].astype(F32) * yc)
    x = x_ref[...] + jnp.dot(merged.astype(BF16), wo_ref[...], preferred_element_type=F32)
    if final:
        x = x * lax.rsqrt(jnp.mean(x * x, axis=-1, keepdims=True) + EPS) * fg_ref[...]
    o_ref[...] = x


def _merge(x2, g, wl, oa, ob, ocf, ocb, cg, wa, wb, wc, wo, fg, tm, final):
    rows = x2.shape[0]
    row = lambda i: (i, 0)
    const = lambda i: (0, 0)
    return pl.pallas_call(
        functools.partial(_merge_kernel, final=final),
        grid=(rows // tm,),
        in_specs=[pl.BlockSpec((tm, D_MODEL), row),
                  pl.BlockSpec((1, D_MODEL), const),
                  pl.BlockSpec((D_MODEL, N_LATE), const),
                  pl.BlockSpec((tm, A_WIDTH), row), pl.BlockSpec((tm, B_WIDTH), row),
                  pl.BlockSpec((tm, C_WIDTH), row), pl.BlockSpec((tm, C_WIDTH), row),
                  pl.BlockSpec((1, LANES), const),
                  pl.BlockSpec((A_WIDTH, D_MODEL), const), pl.BlockSpec((B_WIDTH, D_MODEL), const),
                  pl.BlockSpec((C_WIDTH, D_MODEL), const), pl.BlockSpec((D_MODEL, D_MODEL), const),
                  pl.BlockSpec((1, D_MODEL), const)],
        out_specs=pl.BlockSpec((tm, D_MODEL), row),
        out_shape=jax.ShapeDtypeStruct((rows, D_MODEL), F32),
        scratch_shapes=[pltpu.VMEM((tm, C_WIDTH), BF16), pltpu.VMEM((tm, N_BRANCH * D_MODEL), BF16)],
        compiler_params=_cparams(("parallel",)),
    )(x2, g, wl, oa, ob, ocf, ocb, cg, wa, wb, wc, wo, fg)


def _rope_tables(seq):
    inv = 1.0 / (ROPE_THETA ** (jnp.arange(0, HEAD_DIM, 2, dtype=F32) / HEAD_DIM))
    ang = jnp.arange(seq, dtype=F32)[:, None] * inv[None, :]
    cos = jnp.tile(jnp.cos(ang), (1, LANES // (HEAD_DIM // 2)))
    sin = jnp.tile(jnp.concatenate([-jnp.sin(ang), jnp.sin(ang)], axis=1), (1, LANES // HEAD_DIM))
    return cos, sin


def kernel(x, norm_g, w_in, a_sink, b_lambda, b_subln_g, c_conv_w, c_a_log, c_dt_bias, c_norm_g,
           w_bo_a, w_bo_b, w_bo_c, w_out, final_g):
    bsz, seq, _ = x.shape
    depth = w_in.shape[0]
    rows = bsz * seq
    tm_in = min(512, seq)
    tm_out = min(256, seq)
    tq_a = min(256, seq)
    tq_b = min(512, seq)
    tk_b = min(1024, seq)
    cos, sin = _rope_tables(seq)

    def pair_heads(t, axis):
        shp = t.shape
        t = t.reshape(shp[:axis] + (A_KV_HEADS, A_Q_HEADS // A_KV_HEADS, HEAD_DIM) + shp[axis + 1:])
        return jnp.swapaxes(t, axis, axis + 1).reshape(shp)

    w_in16 = lax.optimization_barrier(w_in.astype(BF16))

    def sec(i, scale=None):
        t = w_in16[:, :, _OFF[i]:_OFF[i + 1]]
        return t if scale is None else t * jnp.asarray(scale, BF16)

    qscale = HEAD_DIM ** -0.5
    w_main = jnp.concatenate(
        [pair_heads(sec(0, qscale), 2), sec(1), sec(2), pair_heads(sec(3), 2), sec(4, qscale),
         sec(5), sec(6), sec(7), sec(8)], axis=2)
    w_late = jnp.concatenate([sec(9), sec(12)], axis=2)
    w_ba = w_in16[:, :, _OFF[10]:_OFF[12]]
    dec = jnp.stack([jnp.concatenate([jnp.zeros((depth, 2 * C_HEADS), F32),
                                      -jnp.exp(c_a_log.astype(F32)).reshape(depth, 2 * C_HEADS)], axis=1),
                     jnp.concatenate([jnp.zeros((depth, 2 * C_HEADS), F32),
                                      c_dt_bias.astype(F32).reshape(depth, 2 * C_HEADS)], axis=1)], axis=1)
    wa = pair_heads(w_bo_a, 1).astype(BF16)
    wb = w_bo_b.astype(BF16)
    wc = w_bo_c.astype(BF16)
    wo = w_out.astype(BF16)

    x2 = x.reshape(rows, D_MODEL)
    for l in range(depth):
        (qa, ka, va, za, qb, kb, vb, zb, cqkv, bg) = _inproj(
            x2, norm_g[l][None], w_main[l], w_ba[l], cos, sin, dec[l], seq, tm_in)

        r3 = lambda t: t.reshape(bsz, seq, t.shape[-1])
        oa = _attn_a(a_sink[l].astype(F32), r3(qa), r3(ka), r3(va), r3(za), tq_a)

        lam_init = 0.8 - 0.6 * math.exp(-0.3 * l)
        bl = b_lambda[l].astype(F32)
        lam = jnp.exp(jnp.sum(bl[0] * bl[1])) - jnp.exp(jnp.sum(bl[2] * bl[3])) + lam_init
        lam_s = jnp.stack([lam, jnp.asarray(1.0 - lam_init, F32)])
        ob = _attn_b(lam_s, r3(qb), r3(kb), r3(vb), r3(zb), b_subln_g[l][None].astype(F32), tq_b, tk_b)

        prep = _delta_prep(cqkv, c_conv_w[l].astype(F32), bg, seq, min(256, seq))
        ocf, ocb = _delta_scan(prep, bsz, seq, 4 if seq % (4 * C_CHUNK) == 0 else 2)

        x2 = _merge(x2, norm_g[l][None], w_late[l], oa.reshape(rows, A_WIDTH), ob.reshape(rows, B_WIDTH),
                    ocf.reshape(rows, C_WIDTH), ocb.reshape(rows, C_WIDTH),
                    c_norm_g[l][None].astype(F32), wa[l], wb[l], wc[l], wo[l],
                    final_g[None].astype(F32), tm_out, l == depth - 1)
    return x2.reshape(bsz, seq, D_MODEL)
```

```python
import functools
import math

import numpy as np
import jax
import jax.numpy as jnp
from jax import lax
from jax.experimental import pallas as pl
from jax.experimental.pallas import tpu as pltpu

F32 = jnp.float32
BF16 = jnp.bfloat16

D_MODEL = 1024
HEAD_DIM = 64
ROPE_THETA = 10000.0
EPS = 1e-6
A_Q_HEADS = 8
A_KV_HEADS = 2
A_WIDTH = A_Q_HEADS * HEAD_DIM
A_KV_WIDTH = A_KV_HEADS * HEAD_DIM
WINDOW = 128
B_HEADS = 4
B_V_DIM = 2 * HEAD_DIM
B_WIDTH = B_HEADS * B_V_DIM
C_HEADS = 4
C_HEAD_DIM = 128
C_WIDTH = C_HEADS * C_HEAD_DIM
C_CONV = 5
C_CHUNK = 64
N_BRANCH = 3
LOG2E = math.log2(math.e)
LANES = 128
SUBLANES = 8
MXU_N = 256
VMEM_LIMIT = 56 * 1024 * 1024

IN_SPLITS = (A_WIDTH, A_KV_WIDTH, A_KV_WIDTH, A_WIDTH,
             B_WIDTH, B_WIDTH, B_WIDTH, B_WIDTH,
             3 * C_WIDTH, C_WIDTH, 2 * C_HEADS, 2 * C_HEADS,
             N_BRANCH * D_MODEL)
_OFF = [0] + [int(o) for o in np.cumsum(IN_SPLITS)]

_SEC = {}
for _name, _width in (('aq', A_WIDTH), ('ak', A_KV_WIDTH), ('av', A_KV_WIDTH), ('az', A_WIDTH),
                      ('bq', B_WIDTH), ('bk', B_WIDTH), ('bv', B_WIDTH), ('bz', B_WIDTH),
                      ('cqkv', 3 * C_WIDTH)):
    _SEC[_name] = (sum(w for _, w in _SEC.values()), _width)
N_PROJ = sum(w for _, w in _SEC.values())
N_BA = 4 * C_HEADS
N_LATE = C_WIDTH + N_BRANCH * D_MODEL


def _cparams(sem):
    return pltpu.CompilerParams(dimension_semantics=sem, vmem_limit_bytes=VMEM_LIMIT)


def _rope(t, cos, sin_signed, first_half):
    swapped = jnp.where(first_half, pltpu.roll(t, LANES - 32, 1), pltpu.roll(t, 32, 1))
    return t * cos + swapped * sin_signed


def _softplus(t):
    return jnp.maximum(t, 0.0) + jnp.log(1.0 + jnp.exp(-jnp.abs(t)))


def _normed_bf16(x, g):
    return (x * lax.rsqrt(jnp.mean(x * x, axis=-1, keepdims=True) + EPS) * g).astype(BF16)


def _silu(t):
    return t * jax.nn.sigmoid(t)


def _project(hb, w_ref, slabs):
    for pair in range(len(slabs) // 2):
        acc = jnp.dot(hb, w_ref[:, pair * MXU_N:(pair + 1) * MXU_N], preferred_element_type=F32)
        for half in range(2):
            ref, j, fn = slabs[2 * pair + half]
            t = acc[:, half * LANES:(half + 1) * LANES]
            ref[:, j * LANES:(j + 1) * LANES] = (t if fn is None else fn(t)).astype(ref.dtype)


def _inproj_kernel(x_ref, g_ref, w_ref, wba_ref, cos_ref, sin_ref, dec_ref,
                   qa_ref, ka_ref, va_ref, za_ref, qb_ref, kb_ref, vb_ref, zb_ref, cqkv_ref, bg_ref):
    hb = _normed_bf16(x_ref[...], g_ref[...])
    cos = cos_ref[...]
    sin = sin_ref[...]
    first_half = (lax.broadcasted_iota(jnp.int32, cos.shape, 1) & 32) == 0

    def rope(t):
        return _rope(t, cos, sin, first_half)

    def rope_q(t):
        return rope(t) * LOG2E

    epilogues = (('aq', qa_ref, rope_q), ('ak', ka_ref, rope), ('av', va_ref, None), ('az', za_ref, _silu),
                 ('bq', qb_ref, rope_q), ('bk', kb_ref, rope), ('bv', vb_ref, None),
                 ('bz', zb_ref, _silu), ('cqkv', cqkv_ref, None))
    slabs = []
    for name, ref, fn in epilogues:
        assert _SEC[name][0] == len(slabs) * LANES
        slabs += [(ref, j, fn) for j in range(_SEC[name][1] // LANES)]
    _project(hb, w_ref, slabs)

    ba = jnp.dot(hb, wba_ref[...], preferred_element_type=F32)
    is_beta = lax.broadcasted_iota(jnp.int32, ba.shape, 1) < 2 * C_HEADS
    dec = dec_ref[...]
    bg_ref[...] = jnp.where(is_beta, jax.nn.sigmoid(ba), dec[0:1, :] * _softplus(ba + dec[1:2, :]))


def _inproj(x2, g, w, wba, cos, sin, dec, seq, tm):
    rows = x2.shape[0]
    nseq = seq // tm
    row = lambda i: (i, 0)
    const = lambda i: (0, 0)
    single = pl.Buffered(1)
    out_widths = [('aq', BF16), ('ak', BF16), ('av', BF16), ('az', BF16), ('bq', BF16), ('bk', BF16),
                  ('bv', BF16), ('bz', BF16), ('cqkv', F32)]
    out_shape = [jax.ShapeDtypeStruct((rows, _SEC[n][1]), dt) for n, dt in out_widths]
    out_specs = [pl.BlockSpec((tm, _SEC[n][1]), row) for n, _ in out_widths]
    out_shape += [jax.ShapeDtypeStruct((rows, N_BA), F32)]
    out_specs += [pl.BlockSpec((tm, N_BA), row)]
    return pl.pallas_call(
        _inproj_kernel,
        grid=(rows // tm,),
        in_specs=[pl.BlockSpec((tm, D_MODEL), row),
                  pl.BlockSpec((1, D_MODEL), const),
                  pl.BlockSpec((D_MODEL, N_PROJ), const, pipeline_mode=single),
                  pl.BlockSpec((D_MODEL, N_BA), const),
                  pl.BlockSpec((tm, LANES), lambda i: (i % nseq, 0)),
                  pl.BlockSpec((tm, LANES), lambda i: (i % nseq, 0)),
                  pl.BlockSpec((2, N_BA), const)],
        out_specs=out_specs,
        out_shape=out_shape,
        compiler_params=_cparams(("parallel",)),
    )(x2, g, w, wba, cos, sin, dec)


def _attn_a_kernel(sink_ref, q_ref, kp_ref, k_ref, kn_ref, vp_ref, v_ref, vn_ref, z_ref, o_ref, *, tq, seq):
    i = pl.program_id(1)
    kcat = jnp.concatenate([kp_ref[0], k_ref[0], kn_ref[0]], axis=0)
    vcat = jnp.concatenate([vp_ref[0], v_ref[0], vn_ref[0]], axis=0)
    lane = lax.broadcasted_iota(jnp.int32, (1, LANES), 1)
    low = lane < HEAD_DIM
    one = jnp.ones_like(vcat)
    v_lo = jnp.where(low, vcat, one)
    v_hi = jnp.where(low, one, vcat)
    nk = tq + 2 * WINDOW
    r = lax.broadcasted_iota(jnp.int32, (tq, nk), 0)
    c = lax.broadcasted_iota(jnp.int32, (tq, nk), 1)
    kabs = i * tq - WINDOW + c
    valid = (jnp.abs(c - WINDOW - r) <= WINDOW) & (kabs >= 0) & (kabs < seq)

    def probs(qm, sink):
        s = lax.dot_general(qm, kcat, (((1,), (1,)), ((), ())), preferred_element_type=F32)
        s = jnp.where(valid, s, -1e30)
        m = jnp.maximum(jnp.max(s, axis=-1, keepdims=True), sink)
        return jnp.exp2(s - m).astype(BF16), jnp.exp2(sink - m)

    half = A_Q_HEADS // 2
    for b in range(half):
        q = q_ref[0, :, b * LANES:(b + 1) * LANES]
        p0, e0 = probs(jnp.where(low, q, jnp.zeros_like(q)), sink_ref[b] * LOG2E)
        p1, e1 = probs(jnp.where(low, jnp.zeros_like(q), q), sink_ref[half + b] * LOG2E)
        r0 = jnp.dot(p0, v_lo, preferred_element_type=F32)
        r1 = jnp.dot(p1, v_hi, preferred_element_type=F32)
        num = jnp.where(low, r0, r1)
        den = (jnp.where(low, pltpu.roll(r0, HEAD_DIM, 1), pltpu.roll(r1, HEAD_DIM, 1))
               + jnp.where(low, e0, e1))
        o_ref[0, :, b * LANES:(b + 1) * LANES] = (
            num / den * z_ref[0, :, b * LANES:(b + 1) * LANES].astype(F32)).astype(BF16)


def _attn_a(sink, q, k, v, z, tq):
    bsz, seq, _ = q.shape
    r = tq // WINDOW
    nwb = seq // WINDOW
    main = lambda b, i: (b, i, 0)
    prev = lambda b, i: (b, jnp.maximum(i * r - 1, 0), 0)
    nxt = lambda b, i: (b, jnp.minimum((i + 1) * r, nwb - 1), 0)
    kv_specs = [pl.BlockSpec((1, WINDOW, LANES), prev), pl.BlockSpec((1, tq, LANES), main),
                pl.BlockSpec((1, WINDOW, LANES), nxt)]
    return pl.pallas_call(
        functools.partial(_attn_a_kernel, tq=tq, seq=seq),
        grid=(bsz, seq // tq),
        in_specs=[pl.BlockSpec(memory_space=pltpu.SMEM), pl.BlockSpec((1, tq, A_WIDTH), main)]
                 + kv_specs + kv_specs + [pl.BlockSpec((1, tq, A_WIDTH), main)],
        out_specs=pl.BlockSpec((1, tq, A_WIDTH), main),
        out_shape=jax.ShapeDtypeStruct((bsz, seq, A_WIDTH), BF16),
        compiler_params=_cparams(("parallel", "parallel")),
    )(sink, q, k, k, k, v, v, v, z)


def _attn_b_kernel(lam_ref, q_ref, k_ref, v_ref, z_ref, g_ref, o_ref, acc_ref, m_ref, s_ref,
                   *, tq, tk, seq):
    q = q_ref[0]
    low = lax.broadcasted_iota(jnp.int32, (1, LANES), 1) < HEAD_DIM
    zero = jnp.zeros_like(q)
    q2 = jnp.concatenate([jnp.where(low, q, zero), jnp.where(low, zero, q)], axis=0)
    acc_ref[...] = jnp.zeros_like(acc_ref)
    m_ref[...] = jnp.full_like(m_ref, -jnp.inf)
    ones = jnp.ones((MXU_N, LANES), BF16)
    nk = seq // tk

    def scores(j):
        return lax.dot_general(q2, k_ref[0, j * tk:(j + 1) * tk, :], (((1,), (1,)), ((), ())),
                               preferred_element_type=F32)

    def update(s_ref, j):
        m_old = m_ref[...]
        m_new = jnp.maximum(m_old, jnp.max(s_ref[...], axis=-1, keepdims=True))
        m_ref[...] = m_new
        pv = None
        for kb in range(tk // MXU_N):
            lo = kb * MXU_N
            p = jnp.concatenate(
                [jnp.exp2(s_ref[:, lo + c * LANES:lo + (c + 1) * LANES] - m_ref[...]).astype(BF16)
                 for c in range(MXU_N // LANES)], axis=1)
            vc = jnp.concatenate([v_ref[0, j * tk + lo:j * tk + lo + MXU_N, :], ones], axis=1)
            d = jnp.dot(p, vc, preferred_element_type=F32)
            pv = d if pv is None else pv + d
        alpha = jnp.exp2(m_old - m_new)
        acc_ref[...] = acc_ref[...] * jnp.concatenate([alpha, alpha], axis=1) + pv

    nbuf = s_ref.shape[0]
    s_ref[0] = scores(0)
    for j in range(nk):
        if j + 1 < nk:
            s_ref[(j + 1) % nbuf] = scores(j + 1)
        update(s_ref.at[j % nbuf], j)
    acc = acc_ref[...]
    o1 = acc[:tq, :LANES] / acc[:tq, LANES:]
    o2 = acc[tq:, :LANES] / acc[tq:, LANES:]
    o = o1 - lam_ref[0] * o2
    o = o * lax.rsqrt(jnp.mean(o * o, axis=-1, keepdims=True) + EPS) * g_ref[...] * lam_ref[1]
    o_ref[0] = (o * z_ref[0].astype(F32)).astype(BF16)


def _attn_b(lam, q, k, v, z, g, tq, tk):
    bsz, seq, _ = q.shape
    assert seq % tk == 0 and seq % tq == 0
    qmap = lambda b, h, i: (b, i, h)
    kmap = lambda b, h, i: (b, 0, h)
    return pl.pallas_call(
        functools.partial(_attn_b_kernel, tq=tq, tk=tk, seq=seq),
        grid=(bsz, B_HEADS, seq // tq),
        in_specs=[pl.BlockSpec(memory_space=pltpu.SMEM),
                  pl.BlockSpec((1, tq, LANES), qmap),
                  pl.BlockSpec((1, seq, LANES), kmap),
                  pl.BlockSpec((1, seq, LANES), kmap),
                  pl.BlockSpec((1, tq, LANES), qmap),
                  pl.BlockSpec((1, LANES), lambda b, h, i: (0, 0))],
        out_specs=pl.BlockSpec((1, tq, LANES), qmap),
        out_shape=jax.ShapeDtypeStruct((bsz, seq, B_WIDTH), BF16),
        scratch_shapes=[pltpu.VMEM((2 * tq, 2 * LANES), F32), pltpu.VMEM((2 * tq, LANES), F32),
                        pltpu.VMEM((2, 2 * tq, tk), F32)],
        compiler_params=_cparams(("parallel", "parallel", "parallel")),
    )(lam, q, k, v, z, g)


def _short_conv(xp_ref, x_ref, xn_ref, w_ref, buf_ref, qkv_ref, *, tc, seq):
    pos = (pl.program_id(0) * tc) % seq
    halo = (C_CONV - 1) // 2
    buf_ref[0:SUBLANES, :] = jnp.where(pos == 0, 0.0, xp_ref[...])
    buf_ref[SUBLANES:SUBLANES + tc, :] = x_ref[...]
    buf_ref[SUBLANES + tc:, :] = jnp.where(pos + tc == seq, 0.0, xn_ref[...])
    for j in range(3 * C_HEADS):
        cols = slice(j * LANES, (j + 1) * LANES)
        y = jnp.zeros((tc, LANES), F32)
        for t in range(C_CONV):
            y = y + buf_ref[SUBLANES - halo + t:SUBLANES - halo + t + tc, cols] * w_ref[t:t + 1, cols]
        y = y * jax.nn.sigmoid(y)
        if j < 2 * C_HEADS:
            y = y * lax.rsqrt(jnp.sum(y * y, axis=-1, keepdims=True) + EPS)
        if j < C_HEADS:
            y = y * (C_HEAD_DIM ** -0.5)
        qkv_ref[:, cols] = y


HC = C_HEADS * C_CHUNK
CHUNK_SHIFT = 6


def _mm(a, b):
    return jnp.dot(a, b, preferred_element_type=F32)


def _split3(x):
    hi = x.astype(BF16)
    r = x - hi.astype(F32)
    mid = r.astype(BF16)
    return hi, mid, (r - mid.astype(F32)).astype(BF16)


def _block_diag(x, mask):
    return jnp.where(mask, jnp.concatenate([x] * C_HEADS, axis=0), jnp.zeros((), x.dtype))


def _delta_prep_kernel(xp_ref, x_ref, xn_ref, cw_ref, bg_ref, w_ref, u_ref, qd_ref, ktt_ref, att_ref, eg_ref,
                       buf_ref, qkv_ref, *, rb, seq):
    c = C_CHUNK
    _short_conv(xp_ref, x_ref, xn_ref, cw_ref, buf_ref, qkv_ref, tc=rb, seq=seq)
    bg = bg_ref[...]
    g3 = _split3(bg)
    rr = lax.broadcasted_iota(jnp.int32, (rb, rb), 0)
    cc = lax.broadcasted_iota(jnp.int32, (rb, rb), 1)
    same = (rr >> CHUNK_SHIFT) == (cc >> CHUNK_SHIFT)
    tri = (jnp.where(same & (rr >= cc), 1.0, 0.0).astype(BF16),
           jnp.where(same & (rr <= cc), 1.0, 0.0).astype(BF16))
    gcum = [_mm(t, g3[0]) + _mm(t, g3[1]) + _mm(t, g3[2]) for t in tri]

    row = lax.broadcasted_iota(jnp.int32, (c, HC), 0)
    lm = lax.broadcasted_iota(jnp.int32, (c, HC), 1) & (c - 1)
    eye_cat = row == lm
    lane_lo = lax.broadcasted_iota(jnp.int32, (c, LANES), 1) < c
    bdmask = ((lax.broadcasted_iota(jnp.int32, (HC, HC), 0) >> CHUNK_SHIFT)
              == (lax.broadcasted_iota(jnp.int32, (HC, HC), 1) >> CHUNK_SHIFT))
    zeros_blk = jnp.zeros((c, LANES), BF16)
    heads = range(C_HEADS)

    def col(a, j):
        return jnp.broadcast_to(a[:, j:j + 1], (c, LANES))

    def spread(cols):
        return jnp.concatenate([jnp.where(lane_lo, cols[0], cols[1]),
                                jnp.where(lane_lo, cols[2], cols[3])], axis=1)

    amats = []
    rhss = []
    for ci in range(rb // c):
        rows = slice(ci * c, (ci + 1) * c)
        q = [qkv_ref[rows, h * LANES:(h + 1) * LANES] for h in heads]
        k = [qkv_ref[rows, C_WIDTH + h * LANES:C_WIDTH + (h + 1) * LANES] for h in heads]
        v = [qkv_ref[rows, 2 * C_WIDTH + h * LANES:2 * C_WIDTH + (h + 1) * LANES] for h in heads]
        kb = [t.astype(BF16) for t in k]
        lhs = jnp.concatenate([jnp.concatenate(kb, axis=1),
                               jnp.concatenate([t.astype(BF16) for t in q], axis=1)], axis=0)
        kmask = jnp.concatenate(
            [jnp.concatenate([kb[h] if j == h else zeros_blk for j in heads], axis=1) for h in heads],
            axis=0)
        gram = lax.dot_general(lhs, kmask, (((1,), (1,)), ((), ())), preferred_element_type=F32)
        kk = gram[:c]
        qk = gram[c:]
        beta_all = bg[rows]
        for d in range(2):
            gc = gcum[d][rows]
            off_b = d * C_HEADS
            off_g = 2 * C_HEADS + d * C_HEADS
            bcol = [col(beta_all, off_b + h) for h in heads]
            gcol = [col(gc, off_g + h) for h in heads]
            g_last = gc[c - 1:c, :] if d == 0 else gc[0:1, :]
            glcol = [jnp.broadcast_to(g_last[:, off_g + h:off_g + h + 1], (c, LANES)) for h in heads]
            gc_sp = spread(gcol)
            gc_row = jnp.sum(jnp.where(eye_cat, gc_sp, 0.0), axis=0, keepdims=True)
            incl = (row >= lm) if d == 0 else (row <= lm)
            strict = (row > lm) if d == 0 else (row < lm)
            decay = jnp.where(incl, jnp.exp(jnp.where(incl, gc_sp - gc_row, 0.0)), 0.0)
            amats.append(jnp.where(strict, spread(bcol) * kk * decay, 0.0))
            egc = [jnp.exp(gcol[h]) for h in heads]
            rhss.append(jnp.concatenate(
                [jnp.concatenate([v[h] * bcol[h], k[h] * (bcol[h] * egc[h])], axis=1) for h in heads],
                axis=0))
            qd_ref[ci, d] = jnp.concatenate([q[h] * egc[h] for h in heads], axis=0).astype(BF16)
            kt = jnp.concatenate([k[h] * jnp.exp(glcol[h] - gcol[h]) for h in heads], axis=0)
            ktt_ref[ci, d] = kt.T.astype(BF16)
            att_ref[ci, d] = jnp.where(incl, qk * decay, 0.0).astype(BF16)
            eg_ref[ci, d] = jnp.exp(jnp.concatenate([glcol[h][0:1, :] for h in heads], axis=0))

    units = range(len(amats))
    tms = [-a for a in amats]
    abs_ = [a.astype(BF16) for a in amats]
    pws = [_mm(abs_[i], _block_diag(abs_[i], bdmask)) for i in units]
    for _ in range(CHUNK_SHIFT - 2):
        pbs = [p.astype(BF16) for p in pws]
        both = [_mm(jnp.concatenate([tms[i].astype(BF16), pbs[i]], axis=0), _block_diag(pbs[i], bdmask))
                for i in units]
        tms = [tms[i] + pws[i] + both[i][:c] for i in units]
        pws = [both[i][c:] for i in units]
    last = [_mm(tms[i].astype(BF16), _block_diag(pws[i].astype(BF16), bdmask)) for i in units]
    tms = [tms[i] + pws[i] + last[i] for i in units]
    uws = [_mm(_block_diag(tms[i].astype(BF16), bdmask), rhss[i].astype(BF16)) for i in units]
    for i in units:
        uw = rhss[i] + uws[i]
        u_ref[i // 2, i % 2] = uw[:, :LANES]
        w_ref[i // 2, i % 2] = uw[:, LANES:].astype(BF16)


def _delta_prep(x2, conv_w, bg2, seq, rb):
    rows, width = x2.shape
    nct = rows // C_CHUNK
    nc = rb // C_CHUNK
    r = rb // SUBLANES
    nb8 = rows // SUBLANES
    blk = lambda i: (i, 0, 0, 0)
    shapes = [((HC, LANES), BF16), ((HC, LANES), F32), ((HC, LANES), BF16),
              ((C_HEAD_DIM, HC), BF16), ((C_CHUNK, HC), BF16), ((C_HEADS, LANES), F32)]
    return pl.pallas_call(
        functools.partial(_delta_prep_kernel, rb=rb, seq=seq),
        grid=(rows // rb,),
        in_specs=[pl.BlockSpec((SUBLANES, width), lambda i: (jnp.maximum(i * r - 1, 0), 0)),
                  pl.BlockSpec((rb, width), lambda i: (i, 0)),
                  pl.BlockSpec((SUBLANES, width), lambda i: (jnp.minimum((i + 1) * r, nb8 - 1), 0)),
                  pl.BlockSpec((C_CONV, width), lambda i: (0, 0)),
                  pl.BlockSpec((rb, N_BA), lambda i: (i, 0))],
        out_specs=[pl.BlockSpec((nc, 2) + s, blk) for s, _ in shapes],
        out_shape=[jax.ShapeDtypeStruct((nct, 2) + s, dt) for s, dt in shapes],
        scratch_shapes=[pltpu.VMEM((rb + 2 * SUBLANES, width), F32), pltpu.VMEM((rb, width), F32)],
        compiler_params=_cparams(("parallel",)),
    )(x2, x2, x2, conv_w, bg2)


def _delta_scan_kernel(*refs, cb, bsz):
    ins = (refs[0:6], refs[6:12])
    outs = refs[12:14]
    state_ref = refs[14]
    c = C_CHUNK

    @pl.when(pl.program_id(0) == 0)
    def _():
        state_ref[...] = jnp.zeros_like(state_ref)

    bdmask = ((lax.broadcasted_iota(jnp.int32, (HC, HC), 0) >> CHUNK_SHIFT)
              == (lax.broadcasted_iota(jnp.int32, (HC, HC), 1) >> CHUNK_SHIFT))
    lane_head = lax.broadcasted_iota(jnp.int32, (1, HC), 1) >> CHUNK_SHIFT
    heads = range(C_HEADS)
    chains = [(b, d) for b in range(bsz) for d in range(2)]
    for step in range(cb):
        ci = {d: step if d == 0 else cb - 1 - step for d in range(2)}
        state, prod = {}, {}
        for b, d in chains:
            w = ins[d][0][b, ci[d]]
            qd = ins[d][2][b, ci[d]]
            base = (b * 2 + d) * C_HEADS
            state[b, d] = [state_ref[base + h] for h in heads]
            prod[b, d] = [_mm(jnp.concatenate([w[h * c:(h + 1) * c], qd[h * c:(h + 1) * c]], axis=0),
                              state[b, d][h].astype(BF16)) for h in heads]
        vnb, att_v = {}, {}
        for b, d in chains:
            ws = jnp.concatenate([p[:c] for p in prod[b, d]], axis=0)
            vnb[b, d] = (ins[d][1][b, ci[d]] - ws).astype(BF16)
            att_v[b, d] = _mm(_block_diag(ins[d][4][b, ci[d]], bdmask), vnb[b, d])
        for b, d in chains:
            ktt = ins[d][3][b, ci[d]]
            eg = ins[d][5][b, ci[d]]
            base = (b * 2 + d) * C_HEADS
            for h in heads:
                kth = jnp.where(lane_head == h, ktt, jnp.zeros((), ktt.dtype))
                state_ref[base + h] = state[b, d][h] * eg[h:h + 1, :] + _mm(kth, vnb[b, d])
                outs[d][b, ci[d] * c:(ci[d] + 1) * c, h * LANES:(h + 1) * LANES] = (
                    prod[b, d][h][c:] + att_v[b, d][h * c:(h + 1) * c])


def _delta_scan(prep, bsz, seq, cb):
    nch = seq // C_CHUNK
    nblk = nch // cb
    arrs = [a.reshape((bsz, nch) + a.shape[1:]) for a in prep]

    def specs(d):
        idx = (lambda n: (0, n, d, 0, 0)) if d == 0 else (lambda n: (0, nblk - 1 - n, d, 0, 0))
        return [pl.BlockSpec((bsz, cb, None) + a.shape[3:], idx) for a in arrs]

    out = jax.ShapeDtypeStruct((bsz, seq, C_WIDTH), F32)
    return pl.pallas_call(
        functools.partial(_delta_scan_kernel, cb=cb, bsz=bsz),
        grid=(nblk,),
        in_specs=specs(0) + specs(1),
        out_specs=[pl.BlockSpec((bsz, cb * C_CHUNK, C_WIDTH), lambda n: (0, n, 0)),
                   pl.BlockSpec((bsz, cb * C_CHUNK, C_WIDTH), lambda n: (0, nblk - 1 - n, 0))],
        out_shape=[out, out],
        scratch_shapes=[pltpu.VMEM((bsz * 2 * C_HEADS, C_HEAD_DIM, C_HEAD_DIM), F32)],
        compiler_params=_cparams(("arbitrary",)),
    )(*arrs, *arrs)


def _merge_kernel(x_ref, g_ref, wl_ref, oa_ref, ob_ref, ocf_ref, ocb_ref, cg_ref, wa_ref, wb_ref, wc_ref,
                  wo_ref, fg_ref, o_ref, zc_ref, gates_ref, *, final):
    hb = _normed_bf16(x_ref[...], g_ref[...])
    slabs = ([(zc_ref, j, _silu) for j in range(C_WIDTH // LANES)]
             + [(gates_ref, j, jax.nn.sigmoid) for j in range(N_BRANCH * D_MODEL // LANES)])
    _project(hb, wl_ref, slabs)
    oc = ocf_ref[...] + ocb_ref[...]
    cg = cg_ref[...]
    parts = []
    for h in range(C_HEADS):
        blk = oc[:, h * LANES:(h + 1) * LANES]
        blk = blk * lax.rsqrt(jnp.mean(blk * blk, axis=-1, keepdims=True) + EPS) * cg
        parts.append((blk * zc_ref[:, h * LANES:(h + 1) * LANES].astype(F32)).astype(BF16))
    ocn = jnp.concatenate(parts, axis=1)
    ya = jnp.dot(oa_ref[...], wa_ref[...], preferred_element_type=F32)
    yb = jnp.dot(ob_ref[...], wb_ref[...], preferred_element_type=F32)
    yc = jnp.dot(ocn, wc_ref[...], preferred_element_type=F32)
    d = D_MODEL
    merged = (gates_ref[:, 0:d].astype(F32) * ya + gates_ref[:, d:2 * d].astype(F32) * yb
              + gates_ref[:, 2 * d:3 * d].astype(F32) * yc)
    x = x_ref[...] + jnp.dot(merged.astype(BF16), wo_ref[...], preferred_element_type=F32)
    if final:
        x = x * lax.rsqrt(jnp.mean(x * x, axis=-1, keepdims=True) + EPS) * fg_ref[...]
    o_ref[...] = x


def _merge(x2, g, wl, oa, ob, ocf, ocb, cg, wa, wb, wc, wo, fg, tm, final):
    rows = x2.shape[0]
    row = lambda i: (i, 0)
    const = lambda i: (0, 0)
    return pl.pallas_call(
        functools.partial(_merge_kernel, final=final),
        grid=(rows // tm,),
        in_specs=[pl.BlockSpec((tm, D_MODEL), row),
                  pl.BlockSpec((1, D_MODEL), const),
                  pl.BlockSpec((D_MODEL, N_LATE), const),
                  pl.BlockSpec((tm, A_WIDTH), row), pl.BlockSpec((tm, B_WIDTH), row),
                  pl.BlockSpec((tm, C_WIDTH), row), pl.BlockSpec((tm, C_WIDTH), row),
                  pl.BlockSpec((1, LANES), const),
                  pl.BlockSpec((A_WIDTH, D_MODEL), const), pl.BlockSpec((B_WIDTH, D_MODEL), const),
                  pl.BlockSpec((C_WIDTH, D_MODEL), const), pl.BlockSpec((D_MODEL, D_MODEL), const),
                  pl.BlockSpec((1, D_MODEL), const)],
        out_specs=pl.BlockSpec((tm, D_MODEL), row),
        out_shape=jax.ShapeDtypeStruct((rows, D_MODEL), F32),
        scratch_shapes=[pltpu.VMEM((tm, C_WIDTH), BF16), pltpu.VMEM((tm, N_BRANCH * D_MODEL), BF16)],
        compiler_params=_cparams(("parallel",)),
    )(x2, g, wl, oa, ob, ocf, ocb, cg, wa, wb, wc, wo, fg)


def _rope_tables(seq):
    inv = 1.0 / (ROPE_THETA ** (jnp.arange(0, HEAD_DIM, 2, dtype=F32) / HEAD_DIM))
    ang = jnp.arange(seq, dtype=F32)[:, None] * inv[None, :]
    cos = jnp.tile(jnp.cos(ang), (1, LANES // (HEAD_DIM // 2)))
    sin = jnp.tile(jnp.concatenate([-jnp.sin(ang), jnp.sin(ang)], axis=1), (1, LANES // HEAD_DIM))
    return cos, sin


def kernel(x, norm_g, w_in, a_sink, b_lambda, b_subln_g, c_conv_w, c_a_log, c_dt_bias, c_norm_g,
           w_bo_a, w_bo_b, w_bo_c, w_out, final_g):
    bsz, seq, _ = x.shape
    depth = w_in.shape[0]
    rows = bsz * seq
    tm_in = min(512, seq)
    tm_out = min(256, seq)
    tq_a = min(256, seq)
    tq_b = min(512, seq)
    tk_b = min(1024, seq)
    cos, sin = _rope_tables(seq)

    def pair_heads(t, axis):
        shp = t.shape
        t = t.reshape(shp[:axis] + (A_KV_HEADS, A_Q_HEADS // A_KV_HEADS, HEAD_DIM) + shp[axis + 1:])
        return jnp.swapaxes(t, axis, axis + 1).reshape(shp)

    w_in16 = lax.optimization_barrier(w_in.astype(BF16))

    def sec(i, scale=None):
        t = w_in16[:, :, _OFF[i]:_OFF[i + 1]]
        return t if scale is None else t * jnp.asarray(scale, BF16)

    qscale = HEAD_DIM ** -0.5
    w_main = jnp.concatenate(
        [pair_heads(sec(0, qscale), 2), sec(1), sec(2), pair_heads(sec(3), 2), sec(4, qscale),
         sec(5), sec(6), sec(7), sec(8)], axis=2)
    w_late = jnp.concatenate([sec(9), sec(12)], axis=2)
    w_ba = w_in16[:, :, _OFF[10]:_OFF[12]]
    dec = jnp.stack([jnp.concatenate([jnp.zeros((depth, 2 * C_HEADS), F32),
                                      -jnp.exp(c_a_log.astype(F32)).reshape(depth, 2 * C_HEADS)], axis=1),
                     jnp.concatenate([jnp.zeros((depth, 2 * C_HEADS), F32),
                                      c_dt_bias.astype(F32).reshape(depth, 2 * C_HEADS)], axis=1)], axis=1)
    wa = pair_heads(w_bo_a, 1).astype(BF16)
    wb = w_bo_b.astype(BF16)
    wc = w_bo_c.astype(BF16)
    wo = w_out.astype(BF16)

    x2 = x.reshape(rows, D_MODEL)
    for l in range(depth):
        (qa, ka, va, za, qb, kb, vb, zb, cqkv, bg) = _inproj(
            x2, norm_g[l][None], w_main[l], w_ba[l], cos, sin, dec[l], seq, tm_in)

        r3 = lambda t: t.reshape(bsz, seq, t.shape[-1])
        oa = _attn_a(a_sink[l].astype(F32), r3(qa), r3(ka), r3(va), r3(za), tq_a)

        lam_init = 0.8 - 0.6 * math.exp(-0.3 * l)
        bl = b_lambda[l].astype(F32)
        lam = jnp.exp(jnp.sum(bl[0] * bl[1])) - jnp.exp(jnp.sum(bl[2] * bl[3])) + lam_init
        lam_s = jnp.stack([lam, jnp.asarray(1.0 - lam_init, F32)])
        ob = _attn_b(lam_s, r3(qb), r3(kb), r3(vb), r3(zb), b_subln_g[l][None].astype(F32), tq_b, tk_b)

        prep = _delta_prep(cqkv, c_conv_w[l].astype(F32), bg, seq, min(256, seq))
        ocf, ocb = _delta_scan(prep, bsz, seq, 8 if seq % (8 * C_CHUNK) == 0 else 2)

        x2 = _merge(x2, norm_g[l][None], w_late[l], oa.reshape(rows, A_WIDTH), ob.reshape(rows, B_WIDTH),
                    ocf.reshape(rows, C_WIDTH), ocb.reshape(rows, C_WIDTH),
                    c_norm_g[l][None].astype(F32), wa[l], wb[l], wc[l], wo[l],
                    final_g[None].astype(F32), tm_out, l == depth - 1)
    return x2.reshape(bsz, seq, D_MODEL)
```
